```python
import math
import jax, jax.numpy as jnp
from jax import lax
import numpy as np

D_MODEL = 1024
BATCH = 4
SEQ = 8192
DEPTH = 1

CHUNK = 64
Q_BLOCK = 128
ATT_HEADS = 4
ATT_HEAD_DIM = 64
ATT_V_DIM = 2 * ATT_HEAD_DIM
ATT_QK_WIDTH = 2 * ATT_HEADS * ATT_HEAD_DIM
ATT_WIDTH = ATT_HEADS * ATT_V_DIM
CONV_WIDTH = D_MODEL // 2
CONV_K = 3
ROPE_THETA = 500000.0
ROPE_DIM = ATT_HEAD_DIM // 4
NORM_EPS = 1e-6
SUBLN_EPS = 1e-5
N_BRANCH = 2
SPLIT_SIZES = (ATT_QK_WIDTH, ATT_QK_WIDTH, ATT_WIDTH, ATT_WIDTH,
               CONV_WIDTH, CONV_WIDTH, CONV_WIDTH, CONV_WIDTH,
               D_MODEL, D_MODEL)
IN_WIDTH = sum(SPLIT_SIZES)

kernel_name = "hybrid_diffattn_shortconv_gated_block"


def rmsnorm(x, w, eps):
    xf = x.astype(jnp.float32)
    xf = xf * lax.rsqrt(jnp.mean(xf * xf, axis=-1, keepdims=True) + eps)
    return (xf * w.astype(jnp.float32)).astype(x.dtype)


def partial_rotary(t, positions):
    half = ROPE_DIM // 2
    freqs = ROPE_THETA ** (-jnp.arange(0, ROPE_DIM, 2, dtype=jnp.float32) / ROPE_DIM)
    ang = positions.astype(jnp.float32)[..., None] * freqs
    cos = jnp.cos(ang)[:, :, None, :].astype(t.dtype)
    sin = jnp.sin(ang)[:, :, None, :].astype(t.dtype)
    t1, t2, rest = t[..., :half], t[..., half:ROPE_DIM], t[..., ROPE_DIM:]
    return jnp.concatenate([t1 * cos - t2 * sin, t2 * cos + t1 * sin, rest], axis=-1)


def diff_attention(q, k, v, positions, lam):
    b, s = q.shape[0], q.shape[1]
    nb = s // Q_BLOCK
    scale = 1.0 / math.sqrt(ATT_HEAD_DIM)
    k_chunk = positions // CHUNK
    q_blocks = jnp.moveaxis(q.reshape(b, nb, Q_BLOCK, 2 * ATT_HEADS, ATT_HEAD_DIM), 1, 0)
    p_blocks = jnp.moveaxis(positions.reshape(b, nb, Q_BLOCK), 1, 0)

    def one_block(args):
        qb, pb = args
        sc = jnp.einsum('bqhd,bkhd->bhqk', qb, k,
                        preferred_element_type=jnp.float32) * scale
        mask = k_chunk[:, None, None, :] <= (pb // CHUNK)[:, None, :, None]
        sc = jnp.where(mask, sc, jnp.finfo(jnp.float32).min)
        pr = jax.nn.softmax(sc, axis=-1).reshape(b, ATT_HEADS, 2, Q_BLOCK, s)
        a = pr[:, :, 0] - lam * pr[:, :, 1]
        return jnp.einsum('bhqk,bkhe->bqhe', a.astype(v.dtype), v)

    out = lax.map(one_block, (q_blocks, p_blocks))
    return jnp.moveaxis(out, 0, 1).reshape(b, s, ATT_HEADS, ATT_V_DIM)


def short_conv(u, w):
    return lax.conv_general_dilated(u, w.astype(u.dtype), window_strides=(1,),
                                    padding=[(CONV_K - 1, 0)],
                                    dimension_numbers=('NWC', 'WIO', 'NWC'),
                                    feature_group_count=CONV_WIDTH)


def setup_inputs(seed: int = 0) -> dict:
    key = jax.random.key(seed)
    ks = jax.random.split(key, 16)
    f32 = jnp.float32
    x = jax.random.normal(ks[0], (BATCH, SEQ, D_MODEL), f32)
    positions = jnp.broadcast_to(jnp.arange(SEQ, dtype=jnp.int32)[None, :], (BATCH, SEQ))
    pre_norm_w = 1.0 + 0.02 * jax.random.normal(ks[1], (DEPTH, D_MODEL), f32)
    w_in = jax.random.normal(ks[2], (DEPTH, D_MODEL, IN_WIDTH), f32) * D_MODEL ** -0.5
    merge_bias = 0.01 * jax.random.normal(ks[3], (DEPTH, N_BRANCH, D_MODEL), f32)
    lambda_q1 = 0.1 * jax.random.normal(ks[4], (DEPTH, ATT_HEAD_DIM), f32)
    lambda_k1 = 0.1 * jax.random.normal(ks[5], (DEPTH, ATT_HEAD_DIM), f32)
    lambda_q2 = 0.1 * jax.random.normal(ks[6], (DEPTH, ATT_HEAD_DIM), f32)
    lambda_k2 = 0.1 * jax.random.normal(ks[7], (DEPTH, ATT_HEAD_DIM), f32)
    subln_w = 1.0 + 0.02 * jax.random.normal(ks[8], (DEPTH, ATT_V_DIM), f32)
    w_att_out = jax.random.normal(ks[9], (DEPTH, ATT_WIDTH, D_MODEL), f32) * ATT_WIDTH ** -0.5
    conv_w = jax.random.normal(ks[10], (DEPTH, CONV_K, 1, CONV_WIDTH), f32) * CONV_K ** -0.5
    w_conv_out = jax.random.normal(ks[11], (DEPTH, CONV_WIDTH, D_MODEL), f32) * CONV_WIDTH ** -0.5
    w_out = jax.random.normal(ks[12], (DEPTH, D_MODEL, D_MODEL), f32) * D_MODEL ** -0.5
    post_norm_w = 1.0 + 0.02 * jax.random.normal(ks[13], (DEPTH, D_MODEL), f32)
    return {"x": x, "positions": positions, "pre_norm_w": pre_norm_w, "w_in": w_in,
            "merge_bias": merge_bias, "lambda_q1": lambda_q1, "lambda_k1": lambda_k1,
            "lambda_q2": lambda_q2, "lambda_k2": lambda_k2, "subln_w": subln_w,
            "w_att_out": w_att_out, "conv_w": conv_w, "w_conv_out": w_conv_out,
            "w_out": w_out, "post_norm_w": post_norm_w}


def reference(x, positions, pre_norm_w, w_in, merge_bias, lambda_q1, lambda_k1,
              lambda_q2, lambda_k2, subln_w, w_att_out, conv_w, w_conv_out,
              w_out, post_norm_w):
    b, s, _ = x.shape
    split_idx = list(np.cumsum(SPLIT_SIZES)[:-1])
    for layer in range(DEPTH):
        lambda_init = 0.8 - 0.6 * math.exp(-0.3 * layer)
        h = rmsnorm(x, pre_norm_w[layer], NORM_EPS)
        proj = h @ w_in[layer]
        (q, k, v, z_att, gb, gc, u, z_conv, g_att, g_conv) = jnp.split(proj, split_idx, axis=-1)

        q = partial_rotary(q.reshape(b, s, 2 * ATT_HEADS, ATT_HEAD_DIM), positions)
        k = partial_rotary(k.reshape(b, s, 2 * ATT_HEADS, ATT_HEAD_DIM), positions)
        v = v.reshape(b, s, ATT_HEADS, ATT_V_DIM)
        lam = (jnp.exp(jnp.sum(lambda_q1[layer].astype(jnp.float32) * lambda_k1[layer].astype(jnp.float32)))
               - jnp.exp(jnp.sum(lambda_q2[layer].astype(jnp.float32) * lambda_k2[layer].astype(jnp.float32)))
               + lambda_init)
        att = diff_attention(q, k, v, positions, lam)
        att = rmsnorm(att, subln_w[layer], SUBLN_EPS) * (1.0 - lambda_init)
        att = att.reshape(b, s, ATT_WIDTH) * jax.nn.silu(z_att)
        y_att = att @ w_att_out[layer]

        cv = gb * short_conv(gc * u, conv_w[layer])
        y_conv = (cv * jax.nn.silu(z_conv)) @ w_conv_out[layer]

        m = (jax.nn.sigmoid(g_att + merge_bias[layer, 0]) * y_att
             + jax.nn.sigmoid(g_conv + merge_bias[layer, 1]) * y_conv)
        x = x + rmsnorm(m @ w_out[layer], post_norm_w[layer], NORM_EPS)
    return x
```

```python
import functools
import math

import jax
import jax.numpy as jnp
from jax import lax
from jax.experimental import pallas as pl
from jax.experimental.pallas import tpu as pltpu

D_MODEL = 1024
CHUNK_SHIFT = 6
HEADS = 4
HEAD_DIM = 64
V_DIM = 2 * HEAD_DIM
QK_WIDTH = 2 * HEADS * HEAD_DIM
ATT_WIDTH = HEADS * V_DIM
CONV_WIDTH = D_MODEL // 2
CONV_K = 3
ROPE_THETA = 500000.0
ROPE_DIM = HEAD_DIM // 4
ROPE_HALF = ROPE_DIM // 2
NORM_EPS = 1e-6
SUBLN_EPS = 1e-5
REST_WIDTH = ATT_WIDTH + 4 * CONV_WIDTH + 2 * D_MODEL

TM = 512
TQ = 256
TK = 512
HALO = 16

VMEM_LIMIT = 56 * 1024 * 1024
NEG = float(jnp.finfo(jnp.float32).min) / 2

_NT = (((1,), (1,)), ((), ()))


def _resident(block_shape, index_map):
    return pl.BlockSpec(block_shape, index_map, pipeline_mode=pl.Buffered(1))


def _proj_kernel(x_ref, pos_ref, freq_ref, prew_ref, wqkT_ref, wvT_ref, wrest_ref,
                 qT_ref, k_ref, vT_ref, rest_ref):
    x = x_ref[...]
    ms = jnp.mean(x * x, axis=-1, keepdims=True)
    h = (x * lax.rsqrt(ms + NORM_EPS)) * prew_ref[...]
    hb = h.astype(jnp.bfloat16)

    qkT = lax.dot_general(wqkT_ref[...], hb, _NT, preferred_element_type=jnp.float32)

    pos = pos_ref[0].astype(jnp.float32)
    ang = freq_ref[...] * pos
    cos = jnp.cos(ang)
    sin = jnp.sin(ang)

    def rot(blk):
        t1 = blk[0:ROPE_HALF]
        t2 = blk[ROPE_HALF:ROPE_DIM]
        return jnp.concatenate(
            [t1 * cos - t2 * sin, t2 * cos + t1 * sin, blk[ROPE_DIM:]], axis=0)

    scale = 1.0 / math.sqrt(HEAD_DIM)
    for hd in range(2 * HEADS):
        r0 = hd * HEAD_DIM
        qT_ref[0, r0:r0 + HEAD_DIM, :] = (rot(qkT[r0:r0 + HEAD_DIM]) * scale).astype(jnp.bfloat16)
    kT = jnp.concatenate(
        [rot(qkT[QK_WIDTH + hd * HEAD_DIM:QK_WIDTH + (hd + 1) * HEAD_DIM])
         for hd in range(2 * HEADS)], axis=0)
    k_ref[0] = kT.T.astype(jnp.bfloat16)

    vT = lax.dot_general(wvT_ref[...], hb, _NT, preferred_element_type=jnp.float32)
    vT_ref[0, 0] = vT.astype(jnp.bfloat16)

    for c in range(REST_WIDTH // 512):
        cs = slice(c * 512, (c + 1) * 512)
        rest_ref[:, cs] = jnp.dot(hb, wrest_ref[:, cs],
                                  preferred_element_type=jnp.float32).astype(jnp.bfloat16)


def _attn_kernel(kmin_ref, kmax_ref, qmin_ref, qmax_ref,
                 qT_ref, k_ref, vT_ref, posr_ref, posc_ref,
                 lq1_ref, lk1_ref, lq2_ref, lk2_ref, sw_ref,
                 o_ref, q2_ref, acc_ref, m_ref, l_ref, *, nq, nk, lambda_init):
    b = pl.program_id(0)
    i = pl.program_id(1)
    qmax_i = qmax_ref[b * nq + i]
    qmin_i = qmin_ref[b * nq + i]

    qc = lax.shift_right_arithmetic(posr_ref[0], CHUNK_SHIFT)
    qc2 = jnp.concatenate([qc, qc], axis=1)

    lam = (jnp.exp(jnp.sum(lq1_ref[...] * lk1_ref[...], axis=-1, keepdims=True))
           - jnp.exp(jnp.sum(lq2_ref[...] * lk2_ref[...], axis=-1, keepdims=True))
           + lambda_init)

    for p in range(HEADS):
        lanes = slice(p * V_DIM, (p + 1) * V_DIM)
        q2_ref[...] = jnp.zeros(q2_ref.shape, q2_ref.dtype)
        q2_ref[0:HEAD_DIM, 0:TQ] = qT_ref[0, p * V_DIM:p * V_DIM + HEAD_DIM, :]
        q2_ref[HEAD_DIM:V_DIM, TQ:2 * TQ] = qT_ref[0, p * V_DIM + HEAD_DIM:(p + 1) * V_DIM, :]
        m_ref[...] = jnp.full(m_ref.shape, NEG, jnp.float32)
        l_ref[...] = jnp.zeros(l_ref.shape, jnp.float32)
        acc_ref[...] = jnp.zeros(acc_ref.shape, jnp.float32)

        def kv_step(j, carry):
            @pl.when(kmin_ref[b * nk + j] <= qmax_i)
            def _():
                k0 = pl.multiple_of(j * TK, TK)
                kt = k_ref[0, pl.ds(k0, TK), lanes]
                s = jnp.dot(kt, q2_ref[...], preferred_element_type=jnp.float32)

                def masked(s):
                    kc = lax.shift_right_arithmetic(posc_ref[0, pl.ds(k0, TK), :], CHUNK_SHIFT)
                    return jnp.where(kc <= qc2, s, NEG)

                s = lax.cond(kmax_ref[b * nk + j] <= qmin_i, lambda s: s, masked, s)

                m_old = m_ref[...]
                m_new = jnp.maximum(m_old, jnp.max(s, axis=0, keepdims=True))
                alpha = jnp.exp(m_old - m_new)
                pr = jnp.exp(s - m_new)
                l_ref[...] = alpha * l_ref[...] + jnp.sum(pr, axis=0, keepdims=True)
                vt = vT_ref[0, j, lanes, :]
                acc_ref[...] = alpha * acc_ref[...] + jnp.dot(
                    vt, pr.astype(jnp.bfloat16), preferred_element_type=jnp.float32)
                m_ref[...] = m_new
            return carry

        lax.fori_loop(0, nk, kv_step, 0)

        o = acc_ref[...] / l_ref[...]
        d = o[:, :TQ] - lam * o[:, TQ:]
        ms = jnp.mean(d * d, axis=0, keepdims=True)
        dn = ((d * lax.rsqrt(ms + SUBLN_EPS)) * sw_ref[...]) * (1.0 - lambda_init)
        o_ref[:, lanes] = dn.T


def _out_kernel(x_ref, att_ref, rest_ref, halo_ref, mb_ref, cw_ref,
                wao_ref, wco_ref, wo_ref, postw_ref, o_ref):
    f32 = jnp.float32
    z_att = rest_ref[:, 0:512].astype(f32)
    gb = rest_ref[:, 512:1024].astype(f32)
    gc = rest_ref[:, 1024:1536].astype(f32)
    u = rest_ref[:, 1536:2048].astype(f32)
    z_conv = rest_ref[:, 2048:2560].astype(f32)
    g_att = rest_ref[:, 2560:3584].astype(f32)
    g_conv = rest_ref[:, 3584:4608].astype(f32)

    a = att_ref[...] * jax.nn.silu(z_att)
    y_att = jnp.dot(a.astype(jnp.bfloat16), wao_ref[...], preferred_element_type=f32)

    cu = gc * u
    first = pl.program_id(1) == 0
    hcu = halo_ref[:, 0:512].astype(f32) * halo_ref[:, 512:1024].astype(f32)
    hcu = jnp.where(first, 0.0, hcu)
    h1 = hcu[HALO - 1:HALO]
    h2 = hcu[HALO - 2:HALO - 1]
    row = lax.broadcasted_iota(jnp.int32, cu.shape, 0)
    r1 = jnp.where(row == 0, h1, pltpu.roll(cu, 1, 0))
    r2 = jnp.where(row == 0, h2, jnp.where(row == 1, h1, pltpu.roll(cu, 2, 0)))
    conv = cw_ref[0:1, :] * r2 + cw_ref[1:2, :] * r1 + cw_ref[2:3, :] * cu
    cv = (gb * conv) * jax.nn.silu(z_conv)
    y_conv = jnp.dot(cv.astype(jnp.bfloat16), wco_ref[...], preferred_element_type=f32)

    m = (jax.nn.sigmoid(g_att + mb_ref[0:1, :]) * y_att
         + jax.nn.sigmoid(g_conv + mb_ref[1:2, :]) * y_conv)
    o = jnp.dot(m.astype(jnp.bfloat16), wo_ref[...], preferred_element_type=f32)
    ms = jnp.mean(o * o, axis=-1, keepdims=True)
    o_ref[...] = x_ref[...] + (o * lax.rsqrt(ms + NORM_EPS)) * postw_ref[...]


def _layer(x2, positions, pre_w, w_in, merge_bias, lq1, lk1, lq2, lk2, subln_w,
           w_att_out, conv_w, w_conv_out, w_out, post_w, lambda_init, batch, seq):
    bf16 = jnp.bfloat16
    n = batch * seq
    nt = seq // TM
    nq = seq // TQ
    nk = seq // TK
    assert seq % TM == 0 and seq % TQ == 0 and TM == TK

    freqs = (ROPE_THETA ** (-jnp.arange(0, ROPE_DIM, 2, dtype=jnp.float32) / ROPE_DIM)
             ).reshape(ROPE_HALF, 1)
    wqkT = w_in[:, :2 * QK_WIDTH].T.astype(bf16)
    wvT = w_in[:, 2 * QK_WIDTH:2 * QK_WIDTH + ATT_WIDTH].T.astype(bf16)
    wrest = w_in[:, 2 * QK_WIDTH + ATT_WIDTH:].astype(bf16)
    pos_row = positions.reshape(batch, 1, seq)
    pos_col = positions.reshape(batch, seq, 1)

    cparams = functools.partial(pltpu.CompilerParams, vmem_limit_bytes=VMEM_LIMIT)

    qT, k, vT, rest = pl.pallas_call(
        _proj_kernel,
        grid=(batch, nt),
        in_specs=[
            pl.BlockSpec((TM, D_MODEL), lambda b, t: (b * nt + t, 0)),
            pl.BlockSpec((1, 1, TM), lambda b, t: (b, 0, t)),
            _resident((ROPE_HALF, 1), lambda b, t: (0, 0)),
            _resident((1, D_MODEL), lambda b, t: (0, 0)),
            _resident((2 * QK_WIDTH, D_MODEL), lambda b, t: (0, 0)),
            _resident((ATT_WIDTH, D_MODEL), lambda b, t: (0, 0)),
            _resident((D_MODEL, REST_WIDTH), lambda b, t: (0, 0)),
        ],
        out_specs=[
            pl.BlockSpec((1, QK_WIDTH, TM), lambda b, t: (b, 0, t)),
            pl.BlockSpec((1, TM, QK_WIDTH), lambda b, t: (b, t, 0)),
            pl.BlockSpec((1, 1, ATT_WIDTH, TM), lambda b, t: (b, t, 0, 0)),
            pl.BlockSpec((TM, REST_WIDTH), lambda b, t: (b * nt + t, 0)),
        ],
        out_shape=[
            jax.ShapeDtypeStruct((batch, QK_WIDTH, seq), bf16),
            jax.ShapeDtypeStruct((batch, seq, QK_WIDTH), bf16),
            jax.ShapeDtypeStruct((batch, nk, ATT_WIDTH, TK), bf16),
            jax.ShapeDtypeStruct((n, REST_WIDTH), bf16),
        ],
        compiler_params=cparams(dimension_semantics=("arbitrary", "arbitrary")),
        name="proj",
    )(x2, pos_row, freqs, pre_w.reshape(1, D_MODEL), wqkT, wvT, wrest)

    chunk = lax.shift_right_arithmetic(positions, CHUNK_SHIFT)
    kch = chunk.reshape(batch, nk, TK)
    qch = chunk.reshape(batch, nq, TQ)
    kmin = kch.min(-1).reshape(-1)
    kmax = kch.max(-1).reshape(-1)
    qmin = qch.min(-1).reshape(-1)
    qmax = qch.max(-1).reshape(-1)

    att = pl.pallas_call(
        functools.partial(_attn_kernel, nq=nq, nk=nk, lambda_init=lambda_init),
        grid_spec=pltpu.PrefetchScalarGridSpec(
            num_scalar_prefetch=4,
            grid=(batch, nq),
            in_specs=[
                pl.BlockSpec((1, QK_WIDTH, TQ), lambda b, i, *_: (b, 0, i)),
                _resident((1, seq, QK_WIDTH), lambda b, i, *_: (b, 0, 0)),
                _resident((1, nk, ATT_WIDTH, TK), lambda b, i, *_: (b, 0, 0, 0)),
                pl.BlockSpec((1, 1, TQ), lambda b, i, *_: (b, 0, i)),
                _resident((1, seq, 1), lambda b, i, *_: (b, 0, 0)),
                _resident((1, HEAD_DIM), lambda b, i, *_: (0, 0)),
                _resident((1, HEAD_DIM), lambda b, i, *_: (0, 0)),
                _resident((1, HEAD_DIM), lambda b, i, *_: (0, 0)),
                _resident((1, HEAD_DIM), lambda b, i, *_: (0, 0)),
                _resident((V_DIM, 1), lambda b, i, *_: (0, 0)),
            ],
            out_specs=pl.BlockSpec((TQ, ATT_WIDTH), lambda b, i, *_: (b * nq + i, 0)),
            scratch_shapes=[
                pltpu.VMEM((V_DIM, 2 * TQ), bf16),
                pltpu.VMEM((V_DIM, 2 * TQ), jnp.float32),
                pltpu.VMEM((1, 2 * TQ), jnp.float32),
                pltpu.VMEM((1, 2 * TQ), jnp.float32),
            ],
        ),
        out_shape=jax.ShapeDtypeStruct((n, ATT_WIDTH), jnp.float32),
        compiler_params=cparams(dimension_semantics=("arbitrary", "arbitrary")),
        name="attn",
    )(kmin, kmax, qmin, qmax, qT, k, vT, pos_row, pos_col,
      lq1.reshape(1, HEAD_DIM), lk1.reshape(1, HEAD_DIM),
      lq2.reshape(1, HEAD_DIM), lk2.reshape(1, HEAD_DIM), subln_w.reshape(V_DIM, 1))

    halo_blocks = TM // HALO
    out = pl.pallas_call(
        _out_kernel,
        grid=(batch, nt),
        in_specs=[
            pl.BlockSpec((TM, D_MODEL), lambda b, t: (b * nt + t, 0)),
            pl.BlockSpec((TM, ATT_WIDTH), lambda b, t: (b * nt + t, 0)),
            pl.BlockSpec((TM, REST_WIDTH), lambda b, t: (b * nt + t, 0)),
            pl.BlockSpec((HALO, 2 * CONV_WIDTH),
                         lambda b, t: (jnp.maximum((b * nt + t) * halo_blocks - 1, 0), 1)),
            _resident((2, D_MODEL), lambda b, t: (0, 0)),
            _resident((CONV_K, CONV_WIDTH), lambda b, t: (0, 0)),
            _resident((ATT_WIDTH, D_MODEL), lambda b, t: (0, 0)),
            _resident((CONV_WIDTH, D_MODEL), lambda b, t: (0, 0)),
            _resident((D_MODEL, D_MODEL), lambda b, t: (0, 0)),
            _resident((1, D_MODEL), lambda b, t: (0, 0)),
        ],
        out_specs=pl.BlockSpec((TM, D_MODEL), lambda b, t: (b * nt + t, 0)),
        out_shape=jax.ShapeDtypeStruct((n, D_MODEL), jnp.float32),
        compiler_params=cparams(dimension_semantics=("arbitrary", "arbitrary")),
        name="outp",
    )(x2, att, rest, rest, merge_bias, conv_w.reshape(CONV_K, CONV_WIDTH),
      w_att_out.astype(bf16), w_conv_out.astype(bf16), w_out.astype(bf16),
      post_w.reshape(1, D_MODEL))
    return out


def kernel(x, positions, pre_norm_w, w_in, merge_bias, lambda_q1, lambda_k1, lambda_q2,
           lambda_k2, subln_w, w_att_out, conv_w, w_conv_out, w_out, post_norm_w):
    batch, seq, _ = x.shape
    depth = w_in.shape[0]
    x2 = x.reshape(batch * seq, D_MODEL)
    for layer in range(depth):
        lambda_init = 0.8 - 0.6 * math.exp(-0.3 * layer)
        x2 = _layer(x2, positions, pre_norm_w[layer], w_in[layer], merge_bias[layer],
                    lambda_q1[layer], lambda_k1[layer], lambda_q2[layer], lambda_k2[layer],
                    subln_w[layer], w_att_out[layer], conv_w[layer], w_conv_out[layer],
                    w_out[layer], post_norm_w[layer], lambda_init, batch, seq)
    return x2.reshape(batch, seq, D_MODEL)
```

```python
import functools
import math

import jax
import jax.numpy as jnp
from jax import lax
from jax.experimental import pallas as pl
from jax.experimental.pallas import tpu as pltpu

D_MODEL = 1024
CHUNK_SHIFT = 6
HEADS = 4
HEAD_DIM = 64
V_DIM = 2 * HEAD_DIM
BF16_ROWS = 16
VX_DIM = V_DIM + BF16_ROWS
QK_WIDTH = 2 * HEADS * HEAD_DIM
ATT_WIDTH = HEADS * V_DIM
CONV_WIDTH = D_MODEL // 2
CONV_K = 3
ROPE_THETA = 500000.0
ROPE_DIM = HEAD_DIM // 4
ROPE_HALF = ROPE_DIM // 2
NORM_EPS = 1e-6
SUBLN_EPS = 1e-5
REST_WIDTH = ATT_WIDTH + 4 * CONV_WIDTH + 2 * D_MODEL

TM = 512
TQ = 256
TK = 512
HALO = BF16_ROWS

VMEM_LIMIT = 56 * 1024 * 1024
NEG = float(jnp.finfo(jnp.float32).min) / 2

_NT = (((1,), (1,)), ((), ()))


def _resident(block_shape, index_map):
    return pl.BlockSpec(block_shape, index_map, pipeline_mode=pl.Buffered(1))


def _proj_kernel(x_ref, pos_ref, freq_ref, prew_ref, wqkT_ref, wvT_ref, wrest_ref,
                 qT_ref, k_ref, vT_ref, rest_ref):
    x = x_ref[...]
    ms = jnp.mean(x * x, axis=-1, keepdims=True)
    h = (x * lax.rsqrt(ms + NORM_EPS)) * prew_ref[...]
    hb = h.astype(jnp.bfloat16)

    qkT = lax.dot_general(wqkT_ref[...], hb, _NT, preferred_element_type=jnp.float32)

    pos = pos_ref[0].astype(jnp.float32)
    ang = freq_ref[...] * pos
    cos = jnp.cos(ang)
    sin = jnp.sin(ang)

    def rot(blk):
        t1 = blk[0:ROPE_HALF]
        t2 = blk[ROPE_HALF:ROPE_DIM]
        return jnp.concatenate(
            [t1 * cos - t2 * sin, t2 * cos + t1 * sin, blk[ROPE_DIM:]], axis=0)

    scale = math.log2(math.e) / math.sqrt(HEAD_DIM)
    for hd in range(2 * HEADS):
        r0 = hd * HEAD_DIM
        qT_ref[0, r0:r0 + HEAD_DIM, :] = (rot(qkT[r0:r0 + HEAD_DIM]) * scale).astype(jnp.bfloat16)
    kT = jnp.concatenate(
        [rot(qkT[QK_WIDTH + hd * HEAD_DIM:QK_WIDTH + (hd + 1) * HEAD_DIM])
         for hd in range(2 * HEADS)], axis=0)
    k = kT.T.astype(jnp.bfloat16)
    for p in range(HEADS):
        k_ref[0, p] = k[:, p * V_DIM:(p + 1) * V_DIM]

    vT = lax.dot_general(wvT_ref[...], hb, _NT, preferred_element_type=jnp.float32)
    row = lax.broadcasted_iota(jnp.int32, (BF16_ROWS, TM), 0)
    ones_rows = jnp.where(row == 0, 1.0, 0.0).astype(jnp.bfloat16)
    for p in range(HEADS):
        vT_ref[0, 0, p * VX_DIM:p * VX_DIM + V_DIM, :] = (
            vT[p * V_DIM:(p + 1) * V_DIM].astype(jnp.bfloat16))
        vT_ref[0, 0, p * VX_DIM + V_DIM:(p + 1) * VX_DIM, :] = ones_rows

    for c in range(REST_WIDTH // 512):
        cs = slice(c * 512, (c + 1) * 512)
        rest_ref[:, cs] = jnp.dot(hb, wrest_ref[:, cs],
                                  preferred_element_type=jnp.float32).astype(jnp.bfloat16)


def _attn_kernel(klist_ref, nunm_ref, ntot_ref,
                 qT_ref, k_ref, vT_ref, posr_ref, posc_ref,
                 lq1_ref, lk1_ref, lq2_ref, lk2_ref, sw_ref,
                 o_ref, q2_ref, acc_ref, m_ref, sa_ref, sb_ref, *, nq, nk, lambda_init):
    qt = pl.program_id(0) * nq + pl.program_id(1)
    n = ntot_ref[qt]
    n_unm = nunm_ref[qt]

    qc = lax.shift_right_arithmetic(posr_ref[0], CHUNK_SHIFT)
    qc2 = jnp.concatenate([qc, qc], axis=1)

    for p in range(HEADS):
        q2_ref[p] = jnp.zeros(q2_ref.shape[1:], q2_ref.dtype)
        q2_ref[p, 0:HEAD_DIM, 0:TQ] = qT_ref[0, p * V_DIM:p * V_DIM + HEAD_DIM, :]
        q2_ref[p, HEAD_DIM:V_DIM, TQ:2 * TQ] = qT_ref[0, p * V_DIM + HEAD_DIM:(p + 1) * V_DIM, :]
        m_ref[p] = jnp.full(m_ref.shape[1:], NEG, jnp.float32)
        acc_ref[p] = jnp.zeros(acc_ref.shape[1:], jnp.float32)

    def score(p, r, s_ref, masked):
        k0 = pl.multiple_of(klist_ref[qt * nk + r] * TK, TK)
        s = jnp.dot(k_ref[0, p, pl.ds(k0, TK), :], q2_ref[p],
                    preferred_element_type=jnp.float32)
        if masked:
            kc = lax.shift_right_arithmetic(posc_ref[0, pl.ds(k0, TK), :], CHUNK_SHIFT)
            s = jnp.where(kc <= qc2, s, NEG)
        s_ref[...] = s
        return jnp.max(s, axis=0, keepdims=True)

    def update(p, r, s_ref, mt):
        j = klist_ref[qt * nk + r]
        m_old = m_ref[p]
        m_new = jnp.maximum(m_old, mt)
        alpha = jnp.exp2(m_old - m_new)
        pr = jnp.exp2(s_ref[...] - m_new).astype(jnp.bfloat16)
        vt = vT_ref[0, j, pl.ds(pl.multiple_of(p * VX_DIM, BF16_ROWS), VX_DIM), :]
        acc_ref[p] = alpha * acc_ref[p] + jnp.dot(vt, pr, preferred_element_type=jnp.float32)
        m_ref[p] = m_new

    def half_step(p, r, mt, s_cur, s_nxt):
        wrap = r + 1 == n
        p2 = jnp.where(wrap, p + 1, p)
        r2 = jnp.where(wrap, 0, r + 1)

        def both(masked):
            mt2 = score(p2, r2, s_nxt, masked)
            update(p, r, s_cur, mt)
            return mt2

        mt2 = lax.cond(r2 >= n_unm, lambda: both(True), lambda: both(False))
        return p2, r2, mt2

    def two_steps(i, carry):
        carry = half_step(*carry, sa_ref, sb_ref)
        return half_step(*carry, sb_ref, sa_ref)

    mt0 = lax.cond(n_unm == 0,
                   lambda: score(0, 0, sa_ref, True), lambda: score(0, 0, sa_ref, False))
    carry = lax.fori_loop(0, (HEADS // 2) * n - 1, two_steps, (0, 0, mt0))
    p_last, r_last, mt_last = half_step(*carry, sa_ref, sb_ref)
    update(p_last, r_last, sb_ref, mt_last)

    lam = (jnp.exp(jnp.sum(lq1_ref[...] * lk1_ref[...], axis=-1, keepdims=True))
           - jnp.exp(jnp.sum(lq2_ref[...] * lk2_ref[...], axis=-1, keepdims=True))
           + lambda_init)
    for p in range(HEADS):
        o = acc_ref[p, 0:V_DIM, :] / acc_ref[p, V_DIM:V_DIM + 1, :]
        d = o[:, :TQ] - lam * o[:, TQ:]
        ms = jnp.mean(d * d, axis=0, keepdims=True)
        dn = ((d * lax.rsqrt(ms + SUBLN_EPS)) * sw_ref[...]) * (1.0 - lambda_init)
        o_ref[:, p * V_DIM:(p + 1) * V_DIM] = dn.T


def _out_kernel(x_ref, att_ref, rest_ref, halo_ref, mb_ref, cw_ref,
                wao_ref, wco_ref, wo_ref, postw_ref, o_ref):
    f32 = jnp.float32
    z_att = rest_ref[:, 0:512].astype(f32)
    gb = rest_ref[:, 512:1024].astype(f32)
    gc = rest_ref[:, 1024:1536].astype(f32)
    u = rest_ref[:, 1536:2048].astype(f32)
    z_conv = rest_ref[:, 2048:2560].astype(f32)
    g_att = rest_ref[:, 2560:3584].astype(f32)
    g_conv = rest_ref[:, 3584:4608].astype(f32)

    a = att_ref[...] * jax.nn.silu(z_att)
    y_att = jnp.dot(a.astype(jnp.bfloat16), wao_ref[...], preferred_element_type=f32)

    cu = gc * u
    first = pl.program_id(1) == 0
    hcu = halo_ref[:, 0:512].astype(f32) * halo_ref[:, 512:1024].astype(f32)
    hcu = jnp.where(first, 0.0, hcu)
    h1 = hcu[HALO - 1:HALO]
    h2 = hcu[HALO - 2:HALO - 1]
    row = lax.broadcasted_iota(jnp.int32, cu.shape, 0)
    r1 = jnp.where(row == 0, h1, pltpu.roll(cu, 1, 0))
    r2 = jnp.where(row == 0, h2, jnp.where(row == 1, h1, pltpu.roll(cu, 2, 0)))
    conv = cw_ref[0:1, :] * r2 + cw_ref[1:2, :] * r1 + cw_ref[2:3, :] * cu
    cv = (gb * conv) * jax.nn.silu(z_conv)
    y_conv = jnp.dot(cv.astype(jnp.bfloat16), wco_ref[...], preferred_element_type=f32)

    m = (jax.nn.sigmoid(g_att + mb_ref[0:1, :]) * y_att
         + jax.nn.sigmoid(g_conv + mb_ref[1:2, :]) * y_conv)
    o = jnp.dot(m.astype(jnp.bfloat16), wo_ref[...], preferred_element_type=f32)
    ms = jnp.mean(o * o, axis=-1, keepdims=True)
    o_ref[...] = x_ref[...] + (o * lax.rsqrt(ms + NORM_EPS)) * postw_ref[...]


def _key_tile_lists(positions, batch, nq, nk):
    chunk = lax.shift_right_arithmetic(positions, CHUNK_SHIFT)
    kch = chunk.reshape(batch, 1, nk, TK)
    qch = chunk.reshape(batch, nq, 1, TQ)
    needed = kch.min(-1) <= qch.max(-1)
    nomask = needed & (kch.max(-1) <= qch.min(-1))
    rank = jnp.where(nomask, 0, jnp.where(needed, 1, 2))
    order = jnp.argsort(rank, axis=-1, stable=True).astype(jnp.int32)

    def count(flag):
        return flag.sum(-1).astype(jnp.int32).reshape(-1)

    return order.reshape(-1), count(nomask), count(needed)


def _layer(x2, positions, pre_w, w_in, merge_bias, lq1, lk1, lq2, lk2, subln_w,
           w_att_out, conv_w, w_conv_out, w_out, post_w, lambda_init, batch, seq):
    bf16 = jnp.bfloat16
    n = batch * seq
    nt = seq // TM
    nq = seq // TQ
    nk = seq // TK
    assert seq % TM == 0 and seq % TQ == 0 and TM == TK

    freqs = (ROPE_THETA ** (-jnp.arange(0, ROPE_DIM, 2, dtype=jnp.float32) / ROPE_DIM)
             ).reshape(ROPE_HALF, 1)
    wqkT = w_in[:, :2 * QK_WIDTH].T.astype(bf16)
    wvT = w_in[:, 2 * QK_WIDTH:2 * QK_WIDTH + ATT_WIDTH].T.astype(bf16)
    wrest = w_in[:, 2 * QK_WIDTH + ATT_WIDTH:].astype(bf16)
    pos_row = positions.reshape(batch, 1, seq)
    pos_col = positions.reshape(batch, seq, 1)

    cparams = functools.partial(pltpu.CompilerParams, vmem_limit_bytes=VMEM_LIMIT)

    qT, k, vT, rest = pl.pallas_call(
        _proj_kernel,
        grid=(batch, nt),
        in_specs=[
            pl.BlockSpec((TM, D_MODEL), lambda b, t: (b * nt + t, 0)),
            pl.BlockSpec((1, 1, TM), lambda b, t: (b, 0, t)),
            _resident((ROPE_HALF, 1), lambda b, t: (0, 0)),
            _resident((1, D_MODEL), lambda b, t: (0, 0)),
            _resident((2 * QK_WIDTH, D_MODEL), lambda b, t: (0, 0)),
            _resident((ATT_WIDTH, D_MODEL), lambda b, t: (0, 0)),
            _resident((D_MODEL, REST_WIDTH), lambda b, t: (0, 0)),
        ],
        out_specs=[
            pl.BlockSpec((1, QK_WIDTH, TM), lambda b, t: (b, 0, t)),
            pl.BlockSpec((1, HEADS, TM, V_DIM), lambda b, t: (b, 0, t, 0)),
            pl.BlockSpec((1, 1, HEADS * VX_DIM, TM), lambda b, t: (b, t, 0, 0)),
            pl.BlockSpec((TM, REST_WIDTH), lambda b, t: (b * nt + t, 0)),
        ],
        out_shape=[
            jax.ShapeDtypeStruct((batch, QK_WIDTH, seq), bf16),
            jax.ShapeDtypeStruct((batch, HEADS, seq, V_DIM), bf16),
            jax.ShapeDtypeStruct((batch, nk, HEADS * VX_DIM, TK), bf16),
            jax.ShapeDtypeStruct((n, REST_WIDTH), bf16),
        ],
        compiler_params=cparams(dimension_semantics=("arbitrary", "arbitrary")),
        name="proj",
    )(x2, pos_row, freqs, pre_w.reshape(1, D_MODEL), wqkT, wvT, wrest)

    klist, n_unm, n_tot = _key_tile_lists(positions, batch, nq, nk)

    att = pl.pallas_call(
        functools.partial(_attn_kernel, nq=nq, nk=nk, lambda_init=lambda_init),
        grid_spec=pltpu.PrefetchScalarGridSpec(
            num_scalar_prefetch=3,
            grid=(batch, nq),
            in_specs=[
                pl.BlockSpec((1, QK_WIDTH, TQ), lambda b, i, *_: (b, 0, i)),
                _resident((1, HEADS, seq, V_DIM), lambda b, i, *_: (b, 0, 0, 0)),
                _resident((1, nk, HEADS * VX_DIM, TK), lambda b, i, *_: (b, 0, 0, 0)),
                pl.BlockSpec((1, 1, TQ), lambda b, i, *_: (b, 0, i)),
                _resident((1, seq, 1), lambda b, i, *_: (b, 0, 0)),
                _resident((1, HEAD_DIM), lambda b, i, *_: (0, 0)),
                _resident((1, HEAD_DIM), lambda b, i, *_: (0, 0)),
                _resident((1, HEAD_DIM), lambda b, i, *_: (0, 0)),
                _resident((1, HEAD_DIM), lambda b, i, *_: (0, 0)),
                _resident((V_DIM, 1), lambda b, i, *_: (0, 0)),
            ],
            out_specs=pl.BlockSpec((TQ, ATT_WIDTH), lambda b, i, *_: (b * nq + i, 0)),
            scratch_shapes=[
                pltpu.VMEM((HEADS, V_DIM, 2 * TQ), bf16),
                pltpu.VMEM((HEADS, VX_DIM, 2 * TQ), jnp.float32),
                pltpu.VMEM((HEADS, 1, 2 * TQ), jnp.float32),
                pltpu.VMEM((TK, 2 * TQ), jnp.float32),
                pltpu.VMEM((TK, 2 * TQ), jnp.float32),
            ],
        ),
        out_shape=jax.ShapeDtypeStruct((n, ATT_WIDTH), jnp.float32),
        compiler_params=cparams(dimension_semantics=("arbitrary", "arbitrary")),
        name="attn",
    )(klist, n_unm, n_tot, qT, k, vT, pos_row, pos_col,
      lq1.reshape(1, HEAD_DIM), lk1.reshape(1, HEAD_DIM),
      lq2.reshape(1, HEAD_DIM), lk2.reshape(1, HEAD_DIM), subln_w.reshape(V_DIM, 1))

    halo_blocks = TM // HALO
    out = pl.pallas_call(
        _out_kernel,
        grid=(batch, nt),
        in_specs=[
            pl.BlockSpec((TM, D_MODEL), lambda b, t: (b * nt + t, 0)),
            pl.BlockSpec((TM, ATT_WIDTH), lambda b, t: (b * nt + t, 0)),
            pl.BlockSpec((TM, REST_WIDTH), lambda b, t: (b * nt + t, 0)),
            pl.BlockSpec((HALO, 2 * CONV_WIDTH),
                         lambda b, t: (jnp.maximum((b * nt + t) * halo_blocks - 1, 0), 1)),
            _resident((2, D_MODEL), lambda b, t: (0, 0)),
            _resident((CONV_K, CONV_WIDTH), lambda b, t: (0, 0)),
            _resident((ATT_WIDTH, D_MODEL), lambda b, t: (0, 0)),
            _resident((CONV_WIDTH, D_MODEL), lambda b, t: (0, 0)),
            _resident((D_MODEL, D_MODEL), lambda b, t: (0, 0)),
            _resident((1, D_MODEL), lambda b, t: (0, 0)),
        ],
        out_specs=pl.BlockSpec((TM, D_MODEL), lambda b, t: (b * nt + t, 0)),
        out_shape=jax.ShapeDtypeStruct((n, D_MODEL), jnp.float32),
        compiler_params=cparams(dimension_semantics=("arbitrary", "arbitrary")),
        name="outp",
    )(x2, att, rest, rest, merge_bias, conv_w.reshape(CONV_K, CONV_WIDTH),
      w_att_out.astype(bf16), w_conv_out.astype(bf16), w_out.astype(bf16),
      post_w.reshape(1, D_MODEL))
    return out


def kernel(x, positions, pre_norm_w, w_in, merge_bias, lambda_q1, lambda_k1, lambda_q2,
           lambda_k2, subln_w, w_att_out, conv_w, w_conv_out, w_out, post_norm_w):
    batch, seq, _ = x.shape
    depth = w_in.shape[0]
    x2 = x.reshape(batch * seq, D_MODEL)
    for layer in range(depth):
        lambda_init = 0.8 - 0.6 * math.exp(-0.3 * layer)
        x2 = _layer(x2, positions, pre_norm_w[layer], w_in[layer], merge_bias[layer],
                    lambda_q1[layer], lambda_k1[layer], lambda_q2[layer], lambda_k2[layer],
                    subln_w[layer], w_att_out[layer], conv_w[layer], w_conv_out[layer],
                    w_out[layer], post_norm_w[layer], lambda_init, batch, seq)
    return x2.reshape(batch, seq, D_MODEL)
```

```python
import functools
import math

import jax
import jax.numpy as jnp
from jax import lax
from jax.experimental import pallas as pl
from jax.experimental.pallas import tpu as pltpu

D_MODEL = 1024
CHUNK_SHIFT = 6
HEADS = 4
HEAD_DIM = 64
V_DIM = 2 * HEAD_DIM
BF16_ROWS = 16
VX_DIM = V_DIM + BF16_ROWS
QK_WIDTH = 2 * HEADS * HEAD_DIM
ATT_WIDTH = HEADS * V_DIM
CONV_WIDTH = D_MODEL // 2
CONV_K = 3
ROPE_THETA = 500000.0
ROPE_DIM = HEAD_DIM // 4
ROPE_HALF = ROPE_DIM // 2
NORM_EPS = 1e-6
SUBLN_EPS = 1e-5
REST_WIDTH = ATT_WIDTH + 4 * CONV_WIDTH + 2 * D_MODEL

TM = 512
TQ = 512
TK = 512
HALO = BF16_ROWS

VMEM_LIMIT = 56 * 1024 * 1024
NEG = float(jnp.finfo(jnp.float32).min) / 2

_NT = (((1,), (1,)), ((), ()))


def _resident(block_shape, index_map):
    return pl.BlockSpec(block_shape, index_map, pipeline_mode=pl.Buffered(1))


def _proj_kernel(x_ref, pos_ref, freq_ref, prew_ref, wqkT_ref, wvT_ref, wrest_ref,
                 qT_ref, k_ref, vT_ref, rest_ref):
    x = x_ref[...]
    ms = jnp.mean(x * x, axis=-1, keepdims=True)
    h = (x * lax.rsqrt(ms + NORM_EPS)) * prew_ref[...]
    hb = h.astype(jnp.bfloat16)

    qkT = lax.dot_general(wqkT_ref[...], hb, _NT, preferred_element_type=jnp.float32)

    pos = pos_ref[0].astype(jnp.float32)
    ang = freq_ref[...] * pos
    cos = jnp.cos(ang)
    sin = jnp.sin(ang)

    def rot(blk):
        t1 = blk[0:ROPE_HALF]
        t2 = blk[ROPE_HALF:ROPE_DIM]
        return jnp.concatenate(
            [t1 * cos - t2 * sin, t2 * cos + t1 * sin, blk[ROPE_DIM:]], axis=0)

    scale = math.log2(math.e) / math.sqrt(HEAD_DIM)
    for hd in range(2 * HEADS):
        r0 = hd * HEAD_DIM
        qT_ref[0, r0:r0 + HEAD_DIM, :] = (rot(qkT[r0:r0 + HEAD_DIM]) * scale).astype(jnp.bfloat16)
    kT = jnp.concatenate(
        [rot(qkT[QK_WIDTH + hd * HEAD_DIM:QK_WIDTH + (hd + 1) * HEAD_DIM])
         for hd in range(2 * HEADS)], axis=0)
    k = kT.T.astype(jnp.bfloat16)
    for p in range(HEADS):
        k_ref[0, p] = k[:, p * V_DIM:(p + 1) * V_DIM]

    vT = lax.dot_general(wvT_ref[...], hb, _NT, preferred_element_type=jnp.float32)
    row = lax.broadcasted_iota(jnp.int32, (BF16_ROWS, TM), 0)
    ones_rows = jnp.where(row == 0, 1.0, 0.0).astype(jnp.bfloat16)
    for p in range(HEADS):
        vT_ref[0, 0, p * VX_DIM:p * VX_DIM + V_DIM, :] = (
            vT[p * V_DIM:(p + 1) * V_DIM].astype(jnp.bfloat16))
        vT_ref[0, 0, p * VX_DIM + V_DIM:(p + 1) * VX_DIM, :] = ones_rows

    for c in range(REST_WIDTH // 512):
        cs = slice(c * 512, (c + 1) * 512)
        rest_ref[:, cs] = jnp.dot(hb, wrest_ref[:, cs],
                                  preferred_element_type=jnp.float32).astype(jnp.bfloat16)


def _attn_kernel(klist_ref, nunm_ref, ntot_ref,
                 qT_ref, k_ref, vT_ref, posr_ref, posc_ref,
                 lq1_ref, lk1_ref, lq2_ref, lk2_ref, sw_ref,
                 o_ref, q2_ref, acc_ref, m_ref, sa_ref, sb_ref, mt0_ref,
                 *, nq, nk, lambda_init):
    qt = pl.program_id(0) * nq + pl.program_id(1)
    n = ntot_ref[qt]
    n_unm = nunm_ref[qt]

    qc = lax.shift_right_arithmetic(posr_ref[0], CHUNK_SHIFT)
    qc2 = jnp.concatenate([qc, qc], axis=1)

    for p in range(HEADS):
        q2_ref[p] = jnp.zeros(q2_ref.shape[1:], q2_ref.dtype)
        q2_ref[p, 0:HEAD_DIM, 0:TQ] = qT_ref[0, p * V_DIM:p * V_DIM + HEAD_DIM, :]
        q2_ref[p, HEAD_DIM:V_DIM, TQ:2 * TQ] = qT_ref[0, p * V_DIM + HEAD_DIM:(p + 1) * V_DIM, :]
        m_ref[p] = jnp.full(m_ref.shape[1:], NEG, jnp.float32)
        acc_ref[p] = jnp.zeros(acc_ref.shape[1:], jnp.float32)

    def key_tile(r):
        return klist_ref[qt * nk + r]

    def score(p, j, s_ref, masked):
        k0 = pl.multiple_of(j * TK, TK)
        s = jnp.dot(k_ref[0, p, pl.ds(k0, TK), :], q2_ref[p],
                    preferred_element_type=jnp.float32)
        if masked:
            kc = lax.shift_right_arithmetic(posc_ref[0, pl.ds(k0, TK), :], CHUNK_SHIFT)
            s = jnp.where(kc <= qc2, s, NEG)
        s_ref[...] = s
        return jnp.max(s, axis=0, keepdims=True)

    def update(p, j, s_ref, mt):
        m_old = m_ref[p]
        m_new = jnp.maximum(m_old, mt)
        alpha = jnp.exp2(m_old - m_new)
        pr = jnp.exp2(s_ref[...] - m_new).astype(jnp.bfloat16)
        vt = vT_ref[0, j, p * VX_DIM:(p + 1) * VX_DIM, :]
        acc_ref[p] = alpha * acc_ref[p] + jnp.dot(vt, pr, preferred_element_type=jnp.float32)
        m_ref[p] = m_new

    def block(j, j_next, masked, next_masked):
        mt1 = score(1, j, sb_ref, masked)
        update(0, j, sa_ref, mt0_ref[...])
        mt2 = score(2, j, sa_ref, masked)
        update(1, j, sb_ref, mt1)
        mt3 = score(3, j, sb_ref, masked)
        update(2, j, sa_ref, mt2)
        if next_masked is not None:
            mt0_ref[...] = score(0, j_next, sa_ref, next_masked)
        update(3, j, sb_ref, mt3)

    @pl.when(n_unm > 0)
    def _():
        mt0_ref[...] = score(0, key_tile(0), sa_ref, False)

    @pl.when(n_unm == 0)
    def _():
        mt0_ref[...] = score(0, key_tile(0), sa_ref, True)

    def free_block(r, carry):
        block(key_tile(r), key_tile(r + 1), False, False)
        return carry

    lax.fori_loop(0, n_unm - 1, free_block, 0)

    @pl.when((n_unm > 0) & (n > n_unm))
    def _():
        block(key_tile(n_unm - 1), key_tile(n_unm), False, True)

    @pl.when((n_unm > 0) & (n == n_unm))
    def _():
        block(key_tile(n_unm - 1), None, False, None)

    def masked_block(r, carry):
        block(key_tile(r), key_tile(r + 1), True, True)
        return carry

    lax.fori_loop(n_unm, n - 1, masked_block, 0)

    @pl.when(n > n_unm)
    def _():
        block(key_tile(n - 1), None, True, None)

    lam = (jnp.exp(jnp.sum(lq1_ref[...] * lk1_ref[...], axis=-1, keepdims=True))
           - jnp.exp(jnp.sum(lq2_ref[...] * lk2_ref[...], axis=-1, keepdims=True))
           + lambda_init)
    for p in range(HEADS):
        o = acc_ref[p, 0:V_DIM, :] / acc_ref[p, V_DIM:V_DIM + 1, :]
        d = o[:, :TQ] - lam * o[:, TQ:]
        ms = jnp.mean(d * d, axis=0, keepdims=True)
        dn = ((d * lax.rsqrt(ms + SUBLN_EPS)) * sw_ref[...]) * (1.0 - lambda_init)
        o_ref[:, p * V_DIM:(p + 1) * V_DIM] = dn.T


def _out_kernel(x_ref, att_ref, rest_ref, halo_ref, mb_ref, cw_ref,
                wao_ref, wco_ref, wo_ref, postw_ref, o_ref):
    f32 = jnp.float32
    z_att = rest_ref[:, 0:512].astype(f32)
    gb = rest_ref[:, 512:1024].astype(f32)
    gc = rest_ref[:, 1024:1536].astype(f32)
    u = rest_ref[:, 1536:2048].astype(f32)
    z_conv = rest_ref[:, 2048:2560].astype(f32)
    g_att = rest_ref[:, 2560:3584].astype(f32)
    g_conv = rest_ref[:, 3584:4608].astype(f32)

    a = att_ref[...] * jax.nn.silu(z_att)
    y_att = jnp.dot(a.astype(jnp.bfloat16), wao_ref[...], preferred_element_type=f32)

    cu = gc * u
    first = pl.program_id(1) == 0
    hcu = halo_ref[:, 0:512].astype(f32) * halo_ref[:, 512:1024].astype(f32)
    hcu = jnp.where(first, 0.0, hcu)
    h1 = hcu[HALO - 1:HALO]
    h2 = hcu[HALO - 2:HALO - 1]
    row = lax.broadcasted_iota(jnp.int32, cu.shape, 0)
    r1 = jnp.where(row == 0, h1, pltpu.roll(cu, 1, 0))
    r2 = jnp.where(row == 0, h2, jnp.where(row == 1, h1, pltpu.roll(cu, 2, 0)))
    conv = cw_ref[0:1, :] * r2 + cw_ref[1:2, :] * r1 + cw_ref[2:3, :] * cu
    cv = (gb * conv) * jax.nn.silu(z_conv)
    y_conv = jnp.dot(cv.astype(jnp.bfloat16), wco_ref[...], preferred_element_type=f32)

    m = (jax.nn.sigmoid(g_att + mb_ref[0:1, :]) * y_att
         + jax.nn.sigmoid(g_conv + mb_ref[1:2, :]) * y_conv)
    o = jnp.dot(m.astype(jnp.bfloat16), wo_ref[...], preferred_element_type=f32)
    ms = jnp.mean(o * o, axis=-1, keepdims=True)
    o_ref[...] = x_ref[...] + (o * lax.rsqrt(ms + NORM_EPS)) * postw_ref[...]


def _key_tile_lists(positions, batch, nq, nk):
    chunk = lax.shift_right_arithmetic(positions, CHUNK_SHIFT)
    kch = chunk.reshape(batch, 1, nk, TK)
    qch = chunk.reshape(batch, nq, 1, TQ)
    needed = kch.min(-1) <= qch.max(-1)
    nomask = needed & (kch.max(-1) <= qch.min(-1))
    rank = jnp.where(nomask, 0, jnp.where(needed, 1, 2))
    order = jnp.argsort(rank, axis=-1, stable=True).astype(jnp.int32)

    def count(flag):
        return flag.sum(-1).astype(jnp.int32).reshape(-1)

    return order.reshape(-1), count(nomask), count(needed)


def _layer(x2, positions, pre_w, w_in, merge_bias, lq1, lk1, lq2, lk2, subln_w,
           w_att_out, conv_w, w_conv_out, w_out, post_w, lambda_init, batch, seq):
    bf16 = jnp.bfloat16
    n = batch * seq
    nt = seq // TM
    nq = seq // TQ
    nk = seq // TK
    assert seq % TM == 0 and seq % TQ == 0 and TM == TK

    freqs = (ROPE_THETA ** (-jnp.arange(0, ROPE_DIM, 2, dtype=jnp.float32) / ROPE_DIM)
             ).reshape(ROPE_HALF, 1)
    wqkT = w_in[:, :2 * QK_WIDTH].T.astype(bf16)
    wvT = w_in[:, 2 * QK_WIDTH:2 * QK_WIDTH + ATT_WIDTH].T.astype(bf16)
    wrest = w_in[:, 2 * QK_WIDTH + ATT_WIDTH:].astype(bf16)
    pos_row = positions.reshape(batch, 1, seq)
    pos_col = positions.reshape(batch, seq, 1)

    cparams = functools.partial(pltpu.CompilerParams, vmem_limit_bytes=VMEM_LIMIT)

    qT, k, vT, rest = pl.pallas_call(
        _proj_kernel,
        grid=(batch, nt),
        in_specs=[
            pl.BlockSpec((TM, D_MODEL), lambda b, t: (b * nt + t, 0)),
            pl.BlockSpec((1, 1, TM), lambda b, t: (b, 0, t)),
            _resident((ROPE_HALF, 1), lambda b, t: (0, 0)),
            _resident((1, D_MODEL), lambda b, t: (0, 0)),
            _resident((2 * QK_WIDTH, D_MODEL), lambda b, t: (0, 0)),
            _resident((ATT_WIDTH, D_MODEL), lambda b, t: (0, 0)),
            _resident((D_MODEL, REST_WIDTH), lambda b, t: (0, 0)),
        ],
        out_specs=[
            pl.BlockSpec((1, QK_WIDTH, TM), lambda b, t: (b, 0, t)),
            pl.BlockSpec((1, HEADS, TM, V_DIM), lambda b, t: (b, 0, t, 0)),
            pl.BlockSpec((1, 1, HEADS * VX_DIM, TM), lambda b, t: (b, t, 0, 0)),
            pl.BlockSpec((TM, REST_WIDTH), lambda b, t: (b * nt + t, 0)),
        ],
        out_shape=[
            jax.ShapeDtypeStruct((batch, QK_WIDTH, seq), bf16),
            jax.ShapeDtypeStruct((batch, HEADS, seq, V_DIM), bf16),
            jax.ShapeDtypeStruct((batch, nk, HEADS * VX_DIM, TK), bf16),
            jax.ShapeDtypeStruct((n, REST_WIDTH), bf16),
        ],
        compiler_params=cparams(dimension_semantics=("arbitrary", "arbitrary")),
        name="proj",
    )(x2, pos_row, freqs, pre_w.reshape(1, D_MODEL), wqkT, wvT, wrest)

    klist, n_unm, n_tot = _key_tile_lists(positions, batch, nq, nk)

    att = pl.pallas_call(
        functools.partial(_attn_kernel, nq=nq, nk=nk, lambda_init=lambda_init),
        grid_spec=pltpu.PrefetchScalarGridSpec(
            num_scalar_prefetch=3,
            grid=(batch, nq),
            in_specs=[
                pl.BlockSpec((1, QK_WIDTH, TQ), lambda b, i, *_: (b, 0, i)),
                _resident((1, HEADS, seq, V_DIM), lambda b, i, *_: (b, 0, 0, 0)),
                _resident((1, nk, HEADS * VX_DIM, TK), lambda b, i, *_: (b, 0, 0, 0)),
                pl.BlockSpec((1, 1, TQ), lambda b, i, *_: (b, 0, i)),
                _resident((1, seq, 1), lambda b, i, *_: (b, 0, 0)),
                _resident((1, HEAD_DIM), lambda b, i, *_: (0, 0)),
                _resident((1, HEAD_DIM), lambda b, i, *_: (0, 0)),
                _resident((1, HEAD_DIM), lambda b, i, *_: (0, 0)),
                _resident((1, HEAD_DIM), lambda b, i, *_: (0, 0)),
                _resident((V_DIM, 1), lambda b, i, *_: (0, 0)),
            ],
            out_specs=pl.BlockSpec((TQ, ATT_WIDTH), lambda b, i, *_: (b * nq + i, 0)),
            scratch_shapes=[
                pltpu.VMEM((HEADS, V_DIM, 2 * TQ), bf16),
                pltpu.VMEM((HEADS, VX_DIM, 2 * TQ), jnp.float32),
                pltpu.VMEM((HEADS, 1, 2 * TQ), jnp.float32),
                pltpu.VMEM((TK, 2 * TQ), jnp.float32),
                pltpu.VMEM((TK, 2 * TQ), jnp.float32),
                pltpu.VMEM((1, 2 * TQ), jnp.float32),
            ],
        ),
        out_shape=jax.ShapeDtypeStruct((n, ATT_WIDTH), jnp.float32),
        compiler_params=cparams(dimension_semantics=("arbitrary", "arbitrary")),
        name="attn",
    )(klist, n_unm, n_tot, qT, k, vT, pos_row, pos_col,
      lq1.reshape(1, HEAD_DIM), lk1.reshape(1, HEAD_DIM),
      lq2.reshape(1, HEAD_DIM), lk2.reshape(1, HEAD_DIM), subln_w.reshape(V_DIM, 1))

    halo_blocks = TM // HALO
    out = pl.pallas_call(
        _out_kernel,
        grid=(batch, nt),
        in_specs=[
            pl.BlockSpec((TM, D_MODEL), lambda b, t: (b * nt + t, 0)),
            pl.BlockSpec((TM, ATT_WIDTH), lambda b, t: (b * nt + t, 0)),
            pl.BlockSpec((TM, REST_WIDTH), lambda b, t: (b * nt + t, 0)),
            pl.BlockSpec((HALO, 2 * CONV_WIDTH),
                         lambda b, t: (jnp.maximum((b * nt + t) * halo_blocks - 1, 0), 1)),
            _resident((2, D_MODEL), lambda b, t: (0, 0)),
            _resident((CONV_K, CONV_WIDTH), lambda b, t: (0, 0)),
            _resident((ATT_WIDTH, D_MODEL), lambda b, t: (0, 0)),
            _resident((CONV_WIDTH, D_MODEL), lambda b, t: (0, 0)),
            _resident((D_MODEL, D_MODEL), lambda b, t: (0, 0)),
            _resident((1, D_MODEL), lambda b, t: (0, 0)),
        ],
        out_specs=pl.BlockSpec((TM, D_MODEL), lambda b, t: (b * nt + t, 0)),
        out_shape=jax.ShapeDtypeStruct((n, D_MODEL), jnp.float32),
        compiler_params=cparams(dimension_semantics=("arbitrary", "arbitrary")),
        name="outp",
    )(x2, att, rest, rest, merge_bias, conv_w.reshape(CONV_K, CONV_WIDTH),
      w_att_out.astype(bf16), w_conv_out.astype(bf16), w_out.astype(bf16),
      post_w.reshape(1, D_MODEL))
    return out


def kernel(x, positions, pre_norm_w, w_in, merge_bias, lambda_q1, lambda_k1, lambda_q2,
           lambda_k2, subln_w, w_att_out, conv_w, w_conv_out, w_out, post_norm_w):
    batch, seq, _ = x.shape
    depth = w_in.shape[0]
    x2 = x.reshape(batch * seq, D_MODEL)
    for layer in range(depth):
        lambda_init = 0.8 - 0.6 * math.exp(-0.3 * layer)
        x2 = _layer(x2, positions, pre_norm_w[layer], w_in[layer], merge_bias[layer],
                    lambda_q1[layer], lambda_k1[layer], lambda_q2[layer], lambda_k2[layer],
                    subln_w[layer], w_att_out[layer], conv_w[layer], w_conv_out[layer],
                    w_out[layer], post_norm_w[layer], lambda_init, batch, seq)
    return x2.reshape(batch, seq, D_MODEL)
```

```python
import functools
import math

import jax
import jax.numpy as jnp
from jax import lax
from jax.experimental import pallas as pl
from jax.experimental.pallas import tpu as pltpu

D_MODEL = 1024
CHUNK_SHIFT = 6
HEADS = 4
HEAD_DIM = 64
V_DIM = 2 * HEAD_DIM
BF16_ROWS = 16
VX_DIM = V_DIM + BF16_ROWS
QK_WIDTH = 2 * HEADS * HEAD_DIM
ATT_WIDTH = HEADS * V_DIM
CONV_WIDTH = D_MODEL // 2
CONV_K = 3
ROPE_THETA = 500000.0
ROPE_DIM = HEAD_DIM // 4
ROPE_HALF = ROPE_DIM // 2
NORM_EPS = 1e-6
SUBLN_EPS = 1e-5
REST_WIDTH = ATT_WIDTH + 4 * CONV_WIDTH + 2 * D_MODEL

TM = 512
TQ = 512
TK = 512
HALO = BF16_ROWS

VMEM_LIMIT = 56 * 1024 * 1024
NEG = float(jnp.finfo(jnp.float32).min) / 2

_NT = (((1,), (1,)), ((), ()))


def _resident(block_shape, index_map):
    return pl.BlockSpec(block_shape, index_map, pipeline_mode=pl.Buffered(1))


def _proj_kernel(x_ref, pos_ref, freq_ref, prew_ref, wqkT_ref, wvT_ref, wrest_ref, mb_ref, cw_ref,
                 qT_ref, k_ref, vT_ref, sz_ref, cvs_ref, gate_ref, cu_prev_ref):
    f32 = jnp.float32

    @pl.when(pl.program_id(1) == 0)
    def _():
        cu_prev_ref[...] = jnp.zeros(cu_prev_ref.shape, f32)

    x = x_ref[...]
    ms = jnp.mean(x * x, axis=-1, keepdims=True)
    h = (x * lax.rsqrt(ms + NORM_EPS)) * prew_ref[...]
    hb = h.astype(jnp.bfloat16)

    qkT = lax.dot_general(wqkT_ref[...], hb, _NT, preferred_element_type=jnp.float32)

    pos = pos_ref[0].astype(jnp.float32)
    ang = freq_ref[...] * pos
    cos = jnp.cos(ang)
    sin = jnp.sin(ang)

    def rot(blk):
        t1 = blk[0:ROPE_HALF]
        t2 = blk[ROPE_HALF:ROPE_DIM]
        return jnp.concatenate(
            [t1 * cos - t2 * sin, t2 * cos + t1 * sin, blk[ROPE_DIM:]], axis=0)

    scale = math.log2(math.e) / math.sqrt(HEAD_DIM)
    for hd in range(2 * HEADS):
        r0 = hd * HEAD_DIM
        qT_ref[0, r0:r0 + HEAD_DIM, :] = (rot(qkT[r0:r0 + HEAD_DIM]) * scale).astype(jnp.bfloat16)
    kT = jnp.concatenate(
        [rot(qkT[QK_WIDTH + hd * HEAD_DIM:QK_WIDTH + (hd + 1) * HEAD_DIM])
         for hd in range(2 * HEADS)], axis=0)
    k = kT.T.astype(jnp.bfloat16)
    for p in range(HEADS):
        k_ref[0, p] = k[:, p * V_DIM:(p + 1) * V_DIM]

    vT = lax.dot_general(wvT_ref[...], hb, _NT, preferred_element_type=jnp.float32)
    row = lax.broadcasted_iota(jnp.int32, (BF16_ROWS, TM), 0)
    ones_rows = jnp.where(row == 0, 1.0, 0.0).astype(jnp.bfloat16)
    for p in range(HEADS):
        vT_ref[0, 0, p * VX_DIM:p * VX_DIM + V_DIM, :] = (
            vT[p * V_DIM:(p + 1) * V_DIM].astype(jnp.bfloat16))
        vT_ref[0, 0, p * VX_DIM + V_DIM:(p + 1) * VX_DIM, :] = ones_rows

    def cols(c0, width):
        return jnp.dot(hb, wrest_ref[:, c0:c0 + width], preferred_element_type=f32)

    cw = CONV_WIDTH
    sz_ref[...] = jax.nn.silu(cols(0, ATT_WIDTH)).astype(jnp.bfloat16)

    c0 = ATT_WIDTH
    gb = cols(c0, cw)
    cu = cols(c0 + cw, cw) * cols(c0 + 2 * cw, cw)
    z_conv = cols(c0 + 3 * cw, cw)

    prev = cu_prev_ref[...]
    h1 = prev[7:8]
    h2 = prev[6:7]
    row = lax.broadcasted_iota(jnp.int32, cu.shape, 0)
    r1 = jnp.where(row == 0, h1, pltpu.roll(cu, 1, 0))
    r2 = jnp.where(row == 0, h2, jnp.where(row == 1, h1, pltpu.roll(cu, 2, 0)))
    conv = cw_ref[0:1, :] * r2 + cw_ref[1:2, :] * r1 + cw_ref[2:3, :] * cu
    cvs_ref[...] = ((gb * conv) * jax.nn.silu(z_conv)).astype(jnp.bfloat16)
    cu_prev_ref[...] = cu[TM - 8:TM]

    c0 = ATT_WIDTH + 4 * cw
    for g in range(2):
        logits = cols(c0 + g * D_MODEL, D_MODEL) + mb_ref[g:g + 1, :]
        gate_ref[:, g * D_MODEL:(g + 1) * D_MODEL] = jax.nn.sigmoid(logits).astype(jnp.bfloat16)


def _attn_kernel(klist_ref, nunm_ref, ntot_ref,
                 qT_ref, k_ref, vT_ref, posr_ref, posc_ref,
                 lq1_ref, lk1_ref, lq2_ref, lk2_ref, sw_ref, sz_ref,
                 o_ref, q2_ref, acc_ref, m_ref, sa_ref, sb_ref, mt0_ref,
                 *, nq, nk, lambda_init):
    qt = pl.program_id(0) * nq + pl.program_id(1)
    n = ntot_ref[qt]
    n_unm = nunm_ref[qt]

    qc = lax.shift_right_arithmetic(posr_ref[0], CHUNK_SHIFT)
    qc2 = jnp.concatenate([qc, qc], axis=1)

    for p in range(HEADS):
        q2_ref[p] = jnp.zeros(q2_ref.shape[1:], q2_ref.dtype)
        q2_ref[p, 0:HEAD_DIM, 0:TQ] = qT_ref[0, p * V_DIM:p * V_DIM + HEAD_DIM, :]
        q2_ref[p, HEAD_DIM:V_DIM, TQ:2 * TQ] = qT_ref[0, p * V_DIM + HEAD_DIM:(p + 1) * V_DIM, :]
        m_ref[p] = jnp.full(m_ref.shape[1:], NEG, jnp.float32)
        acc_ref[p] = jnp.zeros(acc_ref.shape[1:], jnp.float32)

    def key_tile(r):
        return klist_ref[qt * nk + r]

    def score(p, j, s_ref, masked):
        k0 = pl.multiple_of(j * TK, TK)
        s = jnp.dot(k_ref[0, p, pl.ds(k0, TK), :], q2_ref[p],
                    preferred_element_type=jnp.float32)
        if masked:
            kc = lax.shift_right_arithmetic(posc_ref[0, pl.ds(k0, TK), :], CHUNK_SHIFT)
            s = jnp.where(kc <= qc2, s, NEG)
        s_ref[...] = s
        return jnp.max(s, axis=0, keepdims=True)

    def update(p, j, s_ref, mt):
        m_old = m_ref[p]
        m_new = jnp.maximum(m_old, mt)
        alpha = jnp.exp2(m_old - m_new)
        pr = jnp.exp2(s_ref[...] - m_new).astype(jnp.bfloat16)
        vt = vT_ref[0, j, p * VX_DIM:(p + 1) * VX_DIM, :]
        acc_ref[p] = alpha * acc_ref[p] + jnp.dot(vt, pr, preferred_element_type=jnp.float32)
        m_ref[p] = m_new

    def block(j, j_next, masked, next_masked):
        mt1 = score(1, j, sb_ref, masked)
        update(0, j, sa_ref, mt0_ref[...])
        mt2 = score(2, j, sa_ref, masked)
        update(1, j, sb_ref, mt1)
        mt3 = score(3, j, sb_ref, masked)
        update(2, j, sa_ref, mt2)
        if next_masked is not None:
            mt0_ref[...] = score(0, j_next, sa_ref, next_masked)
        update(3, j, sb_ref, mt3)

    @pl.when(n_unm > 0)
    def _():
        mt0_ref[...] = score(0, key_tile(0), sa_ref, False)

    @pl.when(n_unm == 0)
    def _():
        mt0_ref[...] = score(0, key_tile(0), sa_ref, True)

    def free_block(r, carry):
        block(key_tile(r), key_tile(r + 1), False, False)
        return carry

    lax.fori_loop(0, n_unm - 1, free_block, 0)

    @pl.when((n_unm > 0) & (n > n_unm))
    def _():
        block(key_tile(n_unm - 1), key_tile(n_unm), False, True)

    @pl.when((n_unm > 0) & (n == n_unm))
    def _():
        block(key_tile(n_unm - 1), None, False, None)

    def masked_block(r, carry):
        block(key_tile(r), key_tile(r + 1), True, True)
        return carry

    lax.fori_loop(n_unm, n - 1, masked_block, 0)

    @pl.when(n > n_unm)
    def _():
        block(key_tile(n - 1), None, True, None)

    lam = (jnp.exp(jnp.sum(lq1_ref[...] * lk1_ref[...], axis=-1, keepdims=True))
           - jnp.exp(jnp.sum(lq2_ref[...] * lk2_ref[...], axis=-1, keepdims=True))
           + lambda_init)
    for p in range(HEADS):
        o = acc_ref[p, 0:V_DIM, :] / acc_ref[p, V_DIM:V_DIM + 1, :]
        d = o[:, :TQ] - lam * o[:, TQ:]
        ms = jnp.mean(d * d, axis=0, keepdims=True)
        dn = ((d * lax.rsqrt(ms + SUBLN_EPS)) * sw_ref[...]) * (1.0 - lambda_init)
        lanes = slice(p * V_DIM, (p + 1) * V_DIM)
        o_ref[:, lanes] = (dn.T * sz_ref[:, lanes].astype(jnp.float32)).astype(o_ref.dtype)


def _out_kernel(x_ref, att_ref, cvs_ref, gate_ref, wao_ref, wco_ref, wo_ref, postw_ref, o_ref):
    f32 = jnp.float32
    y_att = jnp.dot(att_ref[...], wao_ref[...], preferred_element_type=f32)
    y_conv = jnp.dot(cvs_ref[...], wco_ref[...], preferred_element_type=f32)
    m = (gate_ref[:, 0:D_MODEL].astype(f32) * y_att
         + gate_ref[:, D_MODEL:2 * D_MODEL].astype(f32) * y_conv)
    o = jnp.dot(m.astype(jnp.bfloat16), wo_ref[...], preferred_element_type=f32)
    ms = jnp.mean(o * o, axis=-1, keepdims=True)
    o_ref[...] = x_ref[...] + (o * lax.rsqrt(ms + NORM_EPS)) * postw_ref[...]


def _key_tile_lists(positions, batch, nq, nk):
    chunk = lax.shift_right_arithmetic(positions, CHUNK_SHIFT)
    kch = chunk.reshape(batch, 1, nk, TK)
    qch = chunk.reshape(batch, nq, 1, TQ)
    needed = kch.min(-1) <= qch.max(-1)
    nomask = needed & (kch.max(-1) <= qch.min(-1))
    rank = jnp.where(nomask, 0, jnp.where(needed, 1, 2))
    order = jnp.argsort(rank, axis=-1, stable=True).astype(jnp.int32)

    def count(flag):
        return flag.sum(-1).astype(jnp.int32).reshape(-1)

    return order.reshape(-1), count(nomask), count(needed)


def _layer(x2, positions, pre_w, w_in, merge_bias, lq1, lk1, lq2, lk2, subln_w,
           w_att_out, conv_w, w_conv_out, w_out, post_w, lambda_init, batch, seq):
    bf16 = jnp.bfloat16
    n = batch * seq
    nt = seq // TM
    nq = seq // TQ
    nk = seq // TK
    assert seq % TM == 0 and seq % TQ == 0 and TM == TK

    freqs = (ROPE_THETA ** (-jnp.arange(0, ROPE_DIM, 2, dtype=jnp.float32) / ROPE_DIM)
             ).reshape(ROPE_HALF, 1)
    wqkT = w_in[:, :2 * QK_WIDTH].T.astype(bf16)
    wvT = w_in[:, 2 * QK_WIDTH:2 * QK_WIDTH + ATT_WIDTH].T.astype(bf16)
    wrest = w_in[:, 2 * QK_WIDTH + ATT_WIDTH:].astype(bf16)
    pos_row = positions.reshape(batch, 1, seq)
    pos_col = positions.reshape(batch, seq, 1)

    cparams = functools.partial(pltpu.CompilerParams, vmem_limit_bytes=VMEM_LIMIT)

    def rows(width):
        return pl.BlockSpec((TM, width), lambda b, t: (b * nt + t, 0))

    qT, k, vT, sz, cvs, gates = pl.pallas_call(
        _proj_kernel,
        grid=(batch, nt),
        in_specs=[
            rows(D_MODEL),
            pl.BlockSpec((1, 1, TM), lambda b, t: (b, 0, t)),
            _resident((ROPE_HALF, 1), lambda b, t: (0, 0)),
            _resident((1, D_MODEL), lambda b, t: (0, 0)),
            _resident((2 * QK_WIDTH, D_MODEL), lambda b, t: (0, 0)),
            _resident((ATT_WIDTH, D_MODEL), lambda b, t: (0, 0)),
            _resident((D_MODEL, REST_WIDTH), lambda b, t: (0, 0)),
            _resident((2, D_MODEL), lambda b, t: (0, 0)),
            _resident((CONV_K, CONV_WIDTH), lambda b, t: (0, 0)),
        ],
        out_specs=[
            pl.BlockSpec((1, QK_WIDTH, TM), lambda b, t: (b, 0, t)),
            pl.BlockSpec((1, HEADS, TM, V_DIM), lambda b, t: (b, 0, t, 0)),
            pl.BlockSpec((1, 1, HEADS * VX_DIM, TM), lambda b, t: (b, t, 0, 0)),
            rows(ATT_WIDTH),
            rows(CONV_WIDTH),
            rows(2 * D_MODEL),
        ],
        out_shape=[
            jax.ShapeDtypeStruct((batch, QK_WIDTH, seq), bf16),
            jax.ShapeDtypeStruct((batch, HEADS, seq, V_DIM), bf16),
            jax.ShapeDtypeStruct((batch, nk, HEADS * VX_DIM, TK), bf16),
            jax.ShapeDtypeStruct((n, ATT_WIDTH), bf16),
            jax.ShapeDtypeStruct((n, CONV_WIDTH), bf16),
            jax.ShapeDtypeStruct((n, 2 * D_MODEL), bf16),
        ],
        scratch_shapes=[pltpu.VMEM((8, CONV_WIDTH), jnp.float32)],
        compiler_params=cparams(dimension_semantics=("arbitrary", "arbitrary")),
        name="proj",
    )(x2, pos_row, freqs, pre_w.reshape(1, D_MODEL), wqkT, wvT, wrest,
      merge_bias, conv_w.reshape(CONV_K, CONV_WIDTH))

    klist, n_unm, n_tot = _key_tile_lists(positions, batch, nq, nk)

    att = pl.pallas_call(
        functools.partial(_attn_kernel, nq=nq, nk=nk, lambda_init=lambda_init),
        grid_spec=pltpu.PrefetchScalarGridSpec(
            num_scalar_prefetch=3,
            grid=(batch, nq),
            in_specs=[
                pl.BlockSpec((1, QK_WIDTH, TQ), lambda b, i, *_: (b, 0, i)),
                _resident((1, HEADS, seq, V_DIM), lambda b, i, *_: (b, 0, 0, 0)),
                _resident((1, nk, HEADS * VX_DIM, TK), lambda b, i, *_: (b, 0, 0, 0)),
                pl.BlockSpec((1, 1, TQ), lambda b, i, *_: (b, 0, i)),
                _resident((1, seq, 1), lambda b, i, *_: (b, 0, 0)),
                _resident((1, HEAD_DIM), lambda b, i, *_: (0, 0)),
                _resident((1, HEAD_DIM), lambda b, i, *_: (0, 0)),
                _resident((1, HEAD_DIM), lambda b, i, *_: (0, 0)),
                _resident((1, HEAD_DIM), lambda b, i, *_: (0, 0)),
                _resident((V_DIM, 1), lambda b, i, *_: (0, 0)),
                pl.BlockSpec((TQ, ATT_WIDTH), lambda b, i, *_: (b * nq + i, 0)),
            ],
            out_specs=pl.BlockSpec((TQ, ATT_WIDTH), lambda b, i, *_: (b * nq + i, 0)),
            scratch_shapes=[
                pltpu.VMEM((HEADS, V_DIM, 2 * TQ), bf16),
                pltpu.VMEM((HEADS, VX_DIM, 2 * TQ), jnp.float32),
                pltpu.VMEM((HEADS, 1, 2 * TQ), jnp.float32),
                pltpu.VMEM((TK, 2 * TQ), jnp.float32),
                pltpu.VMEM((TK, 2 * TQ), jnp.float32),
                pltpu.VMEM((1, 2 * TQ), jnp.float32),
            ],
        ),
        out_shape=jax.ShapeDtypeStruct((n, ATT_WIDTH), bf16),
        compiler_params=cparams(dimension_semantics=("arbitrary", "arbitrary")),
        name="attn",
    )(klist, n_unm, n_tot, qT, k, vT, pos_row, pos_col,
      lq1.reshape(1, HEAD_DIM), lk1.reshape(1, HEAD_DIM),
      lq2.reshape(1, HEAD_DIM), lk2.reshape(1, HEAD_DIM), subln_w.reshape(V_DIM, 1), sz)

    out = pl.pallas_call(
        _out_kernel,
        grid=(batch, nt),
        in_specs=[
            rows(D_MODEL),
            rows(ATT_WIDTH),
            rows(CONV_WIDTH),
            rows(2 * D_MODEL),
            _resident((ATT_WIDTH, D_MODEL), lambda b, t: (0, 0)),
            _resident((CONV_WIDTH, D_MODEL), lambda b, t: (0, 0)),
            _resident((D_MODEL, D_MODEL), lambda b, t: (0, 0)),
            _resident((1, D_MODEL), lambda b, t: (0, 0)),
        ],
        out_specs=rows(D_MODEL),
        out_shape=jax.ShapeDtypeStruct((n, D_MODEL), jnp.float32),
        compiler_params=cparams(dimension_semantics=("arbitrary", "arbitrary")),
        name="outp",
    )(x2, att, cvs, gates, w_att_out.astype(bf16), w_conv_out.astype(bf16), w_out.astype(bf16),
      post_w.reshape(1, D_MODEL))
    return out


def kernel(x, positions, pre_norm_w, w_in, merge_bias, lambda_q1, lambda_k1, lambda_q2,
           lambda_k2, subln_w, w_att_out, conv_w, w_conv_out, w_out, post_norm_w):
    batch, seq, _ = x.shape
    depth = w_in.shape[0]
    x2 = x.reshape(batch * seq, D_MODEL)
    for layer in range(depth):
        lambda_init = 0.8 - 0.6 * math.exp(-0.3 * layer)
        x2 = _layer(x2, positions, pre_norm_w[layer], w_in[layer], merge_bias[layer],
                    lambda_q1[layer], lambda_k1[layer], lambda_q2[layer], lambda_k2[layer],
                    subln_w[layer], w_att_out[layer], conv_w[layer], w_conv_out[layer],
                    w_out[layer], post_norm_w[layer], lambda_init, batch, seq)
    return x2.reshape(batch, seq, D_MODEL)
```

```python
import functools
import math

import jax
import jax.numpy as jnp
from jax import lax
from jax.experimental import pallas as pl
from jax.experimental.pallas import tpu as pltpu

D_MODEL = 1024
CHUNK_SHIFT = 6
HEADS = 4
HEAD_DIM = 64
V_DIM = 2 * HEAD_DIM
BF16_ROWS = 16
VX_DIM = V_DIM + BF16_ROWS
QK_WIDTH = 2 * HEADS * HEAD_DIM
ATT_WIDTH = HEADS * V_DIM
CONV_WIDTH = D_MODEL // 2
CONV_K = 3
ROPE_THETA = 500000.0
ROPE_DIM = HEAD_DIM // 4
ROPE_HALF = ROPE_DIM // 2
NORM_EPS = 1e-6
SUBLN_EPS = 1e-5
REST_WIDTH = ATT_WIDTH + 4 * CONV_WIDTH + 2 * D_MODEL

TM = 512
TQ = 512
TK = 512
OUT_ROWS = 256

VMEM_LIMIT = 56 * 1024 * 1024
NEG = float(jnp.finfo(jnp.float32).min) / 2

_NT = (((1,), (1,)), ((), ()))


def _resident(block_shape, index_map):
    return pl.BlockSpec(block_shape, index_map, pipeline_mode=pl.Buffered(1))


def _proj_kernel(x_ref, pos_ref, freq_ref, prew_ref, wqkT_ref, wvT_ref, wrest_ref, mb_ref, cw_ref,
                 qT_ref, k_ref, vT_ref, sz_ref, cvs_ref, gate_ref, cu_prev_ref):
    f32 = jnp.float32

    @pl.when(pl.program_id(1) == 0)
    def _():
        cu_prev_ref[...] = jnp.zeros(cu_prev_ref.shape, f32)

    x = x_ref[...]
    ms = jnp.mean(x * x, axis=-1, keepdims=True)
    h = (x * lax.rsqrt(ms + NORM_EPS)) * prew_ref[...]
    hb = h.astype(jnp.bfloat16)

    def sigmoid(v):
        return 0.5 * jnp.tanh(0.5 * v) + 0.5

    def cols(c0, width):
        return jnp.dot(hb, wrest_ref[:, c0:c0 + width], preferred_element_type=f32)

    cw = CONV_WIDTH
    c0 = ATT_WIDTH + 4 * cw
    for c in range(0, 2 * D_MODEL, 512):
        bias = mb_ref[c // D_MODEL:c // D_MODEL + 1, c % D_MODEL:c % D_MODEL + 512]
        gate_ref[:, c:c + 512] = sigmoid(cols(c0 + c, 512) + bias).astype(jnp.bfloat16)

    z_att = cols(0, ATT_WIDTH)
    sz_ref[...] = (z_att * sigmoid(z_att)).astype(jnp.bfloat16)

    c0 = ATT_WIDTH
    gb = cols(c0, cw)
    cu = cols(c0 + cw, cw) * cols(c0 + 2 * cw, cw)
    z_conv = cols(c0 + 3 * cw, cw)

    prev = cu_prev_ref[...]
    h1 = prev[7:8]
    h2 = prev[6:7]
    row = lax.broadcasted_iota(jnp.int32, (8, cw), 0)

    def shifted(by, fix_top):
        r = pltpu.roll(cu, by, 0)
        return jnp.concatenate([fix_top(r[0:8]), r[8:]], axis=0)

    r1 = shifted(1, lambda top: jnp.where(row == 0, h1, top))
    r2 = shifted(2, lambda top: jnp.where(row == 0, h2, jnp.where(row == 1, h1, top)))
    conv = cw_ref[0:1, :] * r2 + cw_ref[1:2, :] * r1 + cw_ref[2:3, :] * cu
    cvs_ref[...] = ((gb * conv) * (z_conv * sigmoid(z_conv))).astype(jnp.bfloat16)
    cu_prev_ref[...] = cu[TM - 8:TM]

    qkT = lax.dot_general(wqkT_ref[...], hb, _NT, preferred_element_type=jnp.float32)

    pos = pos_ref[0].astype(jnp.float32)
    ang = freq_ref[...] * pos
    cos = jnp.cos(ang)
    sin = jnp.sin(ang)

    def rot(blk):
        t1 = blk[0:ROPE_HALF]
        t2 = blk[ROPE_HALF:ROPE_DIM]
        return jnp.concatenate(
            [t1 * cos - t2 * sin, t2 * cos + t1 * sin, blk[ROPE_DIM:]], axis=0)

    scale = math.log2(math.e) / math.sqrt(HEAD_DIM)
    for hd in range(2 * HEADS):
        r0 = hd * HEAD_DIM
        qT_ref[0, r0:r0 + HEAD_DIM, :] = (rot(qkT[r0:r0 + HEAD_DIM]) * scale).astype(jnp.bfloat16)
    kT = jnp.concatenate(
        [rot(qkT[QK_WIDTH + hd * HEAD_DIM:QK_WIDTH + (hd + 1) * HEAD_DIM])
         for hd in range(2 * HEADS)], axis=0)
    k = kT.T.astype(jnp.bfloat16)
    for p in range(HEADS):
        k_ref[0, p] = k[:, p * V_DIM:(p + 1) * V_DIM]

    vT = lax.dot_general(wvT_ref[...], hb, _NT, preferred_element_type=jnp.float32)
    row = lax.broadcasted_iota(jnp.int32, (BF16_ROWS, TM), 0)
    ones_rows = jnp.where(row == 0, 1.0, 0.0).astype(jnp.bfloat16)
    for p in range(HEADS):
        vT_ref[0, 0, p * VX_DIM:p * VX_DIM + V_DIM, :] = (
            vT[p * V_DIM:(p + 1) * V_DIM].astype(jnp.bfloat16))
        vT_ref[0, 0, p * VX_DIM + V_DIM:(p + 1) * VX_DIM, :] = ones_rows


def _attn_kernel(klist_ref, nunm_ref, ntot_ref,
                 qT_ref, k_ref, vT_ref, posr_ref, posc_ref,
                 lq1_ref, lk1_ref, lq2_ref, lk2_ref, sw_ref, sz_ref,
                 o_ref, q2_ref, acc_ref, m_ref, sa_ref, sb_ref, mt0_ref,
                 *, nq, nk, lambda_init):
    qt = pl.program_id(0) * nq + pl.program_id(1)
    n = ntot_ref[qt]
    n_unm = nunm_ref[qt]

    qc = lax.shift_right_arithmetic(posr_ref[0], CHUNK_SHIFT)
    qc2 = jnp.concatenate([qc, qc], axis=1)

    for p in range(HEADS):
        q2_ref[p] = jnp.zeros(q2_ref.shape[1:], q2_ref.dtype)
        q2_ref[p, 0:HEAD_DIM, 0:TQ] = qT_ref[0, p * V_DIM:p * V_DIM + HEAD_DIM, :]
        q2_ref[p, HEAD_DIM:V_DIM, TQ:2 * TQ] = qT_ref[0, p * V_DIM + HEAD_DIM:(p + 1) * V_DIM, :]
        m_ref[p] = jnp.full(m_ref.shape[1:], NEG, jnp.float32)
        acc_ref[p] = jnp.zeros(acc_ref.shape[1:], jnp.float32)

    def key_tile(r):
        return klist_ref[qt * nk + r]

    def score(p, j, s_ref, masked):
        k0 = pl.multiple_of(j * TK, TK)
        s = jnp.dot(k_ref[0, p, pl.ds(k0, TK), :], q2_ref[p],
                    preferred_element_type=jnp.float32)
        if masked:
            kc = lax.shift_right_arithmetic(posc_ref[0, pl.ds(k0, TK), :], CHUNK_SHIFT)
            s = jnp.where(kc <= qc2, s, NEG)
        s_ref[...] = s
        return jnp.max(s, axis=0, keepdims=True)

    def update(p, j, s_ref, mt):
        m_old = m_ref[p]
        m_new = jnp.maximum(m_old, mt)
        alpha = jnp.exp2(m_old - m_new)
        pr = jnp.exp2(s_ref[...] - m_new).astype(jnp.bfloat16)
        vt = vT_ref[0, j, p * VX_DIM:(p + 1) * VX_DIM, :]
        acc_ref[p] = alpha * acc_ref[p] + jnp.dot(vt, pr, preferred_element_type=jnp.float32)
        m_ref[p] = m_new

    def block(j, j_next, masked, next_masked):
        mt1 = score(1, j, sb_ref, masked)
        update(0, j, sa_ref, mt0_ref[...])
        mt2 = score(2, j, sa_ref, masked)
        update(1, j, sb_ref, mt1)
        mt3 = score(3, j, sb_ref, masked)
        update(2, j, sa_ref, mt2)
        if next_masked is not None:
            mt0_ref[...] = score(0, j_next, sa_ref, next_masked)
        update(3, j, sb_ref, mt3)

    @pl.when(n_unm > 0)
    def _():
        mt0_ref[...] = score(0, key_tile(0), sa_ref, False)

    @pl.when(n_unm == 0)
    def _():
        mt0_ref[...] = score(0, key_tile(0), sa_ref, True)

    def free_block(r, carry):
        block(key_tile(r), key_tile(r + 1), False, False)
        return carry

    lax.fori_loop(0, n_unm - 1, free_block, 0)

    @pl.when((n_unm > 0) & (n > n_unm))
    def _():
        block(key_tile(n_unm - 1), key_tile(n_unm), False, True)

    @pl.when((n_unm > 0) & (n == n_unm))
    def _():
        block(key_tile(n_unm - 1), None, False, None)

    def masked_block(r, carry):
        block(key_tile(r), key_tile(r + 1), True, True)
        return carry

    lax.fori_loop(n_unm, n - 1, masked_block, 0)

    @pl.when(n > n_unm)
    def _():
        block(key_tile(n - 1), None, True, None)

    lam = (jnp.exp(jnp.sum(lq1_ref[...] * lk1_ref[...], axis=-1, keepdims=True))
           - jnp.exp(jnp.sum(lq2_ref[...] * lk2_ref[...], axis=-1, keepdims=True))
           + lambda_init)
    for p in range(HEADS):
        o = acc_ref[p, 0:V_DIM, :] / acc_ref[p, V_DIM:V_DIM + 1, :]
        d = o[:, :TQ] - lam * o[:, TQ:]
        ms = jnp.mean(d * d, axis=0, keepdims=True)
        dn = ((d * lax.rsqrt(ms + SUBLN_EPS)) * sw_ref[...]) * (1.0 - lambda_init)
        lanes = slice(p * V_DIM, (p + 1) * V_DIM)
        o_ref[:, lanes] = (dn.T * sz_ref[:, lanes].astype(jnp.float32)).astype(o_ref.dtype)


def _out_kernel(x_ref, att_ref, cvs_ref, gate_ref, wao_ref, wco_ref, wo_ref, postw_ref, o_ref):
    f32 = jnp.float32
    for r0 in range(0, TM, OUT_ROWS):
        rows = slice(r0, r0 + OUT_ROWS)
        y_att = jnp.dot(att_ref[rows, :], wao_ref[...], preferred_element_type=f32)
        y_conv = jnp.dot(cvs_ref[rows, :], wco_ref[...], preferred_element_type=f32)
        m = (gate_ref[rows, 0:D_MODEL].astype(f32) * y_att
             + gate_ref[rows, D_MODEL:2 * D_MODEL].astype(f32) * y_conv)
        o = jnp.dot(m.astype(jnp.bfloat16), wo_ref[...], preferred_element_type=f32)
        ms = jnp.mean(o * o, axis=-1, keepdims=True)
        o_ref[rows, :] = x_ref[rows, :] + (o * lax.rsqrt(ms + NORM_EPS)) * postw_ref[...]


def _key_tile_lists(positions, batch, nq, nk):
    chunk = lax.shift_right_arithmetic(positions, CHUNK_SHIFT)
    kch = chunk.reshape(batch, 1, nk, TK)
    qch = chunk.reshape(batch, nq, 1, TQ)
    needed = kch.min(-1) <= qch.max(-1)
    nomask = needed & (kch.max(-1) <= qch.min(-1))
    rank = jnp.where(nomask, 0, jnp.where(needed, 1, 2))
    order = jnp.argsort(rank, axis=-1, stable=True).astype(jnp.int32)

    def count(flag):
        return flag.sum(-1).astype(jnp.int32).reshape(-1)

    return order.reshape(-1), count(nomask), count(needed)


def _layer(x2, positions, pre_w, w_in, merge_bias, lq1, lk1, lq2, lk2, subln_w,
           w_att_out, conv_w, w_conv_out, w_out, post_w, lambda_init, batch, seq):
    bf16 = jnp.bfloat16
    n = batch * seq
    nt = seq // TM
    nq = seq // TQ
    nk = seq // TK
    assert seq % TM == 0 and seq % TQ == 0 and TM == TK

    freqs = (ROPE_THETA ** (-jnp.arange(0, ROPE_DIM, 2, dtype=jnp.float32) / ROPE_DIM)
             ).reshape(ROPE_HALF, 1)
    wqkT = w_in[:, :2 * QK_WIDTH].T.astype(bf16)
    wvT = w_in[:, 2 * QK_WIDTH:2 * QK_WIDTH + ATT_WIDTH].T.astype(bf16)
    wrest = w_in[:, 2 * QK_WIDTH + ATT_WIDTH:].astype(bf16)
    pos_row = positions.reshape(batch, 1, seq)
    pos_col = positions.reshape(batch, seq, 1)

    cparams = functools.partial(pltpu.CompilerParams, vmem_limit_bytes=VMEM_LIMIT)

    def rows(width):
        return pl.BlockSpec((TM, width), lambda b, t: (b * nt + t, 0))

    qT, k, vT, sz, cvs, gates = pl.pallas_call(
        _proj_kernel,
        grid=(batch, nt),
        in_specs=[
            rows(D_MODEL),
            pl.BlockSpec((1, 1, TM), lambda b, t: (b, 0, t)),
            _resident((ROPE_HALF, 1), lambda b, t: (0, 0)),
            _resident((1, D_MODEL), lambda b, t: (0, 0)),
            _resident((2 * QK_WIDTH, D_MODEL), lambda b, t: (0, 0)),
            _resident((ATT_WIDTH, D_MODEL), lambda b, t: (0, 0)),
            _resident((D_MODEL, REST_WIDTH), lambda b, t: (0, 0)),
            _resident((2, D_MODEL), lambda b, t: (0, 0)),
            _resident((CONV_K, CONV_WIDTH), lambda b, t: (0, 0)),
        ],
        out_specs=[
            pl.BlockSpec((1, QK_WIDTH, TM), lambda b, t: (b, 0, t)),
            pl.BlockSpec((1, HEADS, TM, V_DIM), lambda b, t: (b, 0, t, 0)),
            pl.BlockSpec((1, 1, HEADS * VX_DIM, TM), lambda b, t: (b, t, 0, 0)),
            rows(ATT_WIDTH),
            rows(CONV_WIDTH),
            rows(2 * D_MODEL),
        ],
        out_shape=[
            jax.ShapeDtypeStruct((batch, QK_WIDTH, seq), bf16),
            jax.ShapeDtypeStruct((batch, HEADS, seq, V_DIM), bf16),
            jax.ShapeDtypeStruct((batch, nk, HEADS * VX_DIM, TK), bf16),
            jax.ShapeDtypeStruct((n, ATT_WIDTH), bf16),
            jax.ShapeDtypeStruct((n, CONV_WIDTH), bf16),
            jax.ShapeDtypeStruct((n, 2 * D_MODEL), bf16),
        ],
        scratch_shapes=[pltpu.VMEM((8, CONV_WIDTH), jnp.float32)],
        compiler_params=cparams(dimension_semantics=("arbitrary", "arbitrary")),
        name="proj",
    )(x2, pos_row, freqs, pre_w.reshape(1, D_MODEL), wqkT, wvT, wrest,
      merge_bias, conv_w.reshape(CONV_K, CONV_WIDTH))

    klist, n_unm, n_tot = _key_tile_lists(positions, batch, nq, nk)

    att = pl.pallas_call(
        functools.partial(_attn_kernel, nq=nq, nk=nk, lambda_init=lambda_init),
        grid_spec=pltpu.PrefetchScalarGridSpec(
            num_scalar_prefetch=3,
            grid=(batch, nq),
            in_specs=[
                pl.BlockSpec((1, QK_WIDTH, TQ), lambda b, i, *_: (b, 0, i)),
                _resident((1, HEADS, seq, V_DIM), lambda b, i, *_: (b, 0, 0, 0)),
                _resident((1, nk, HEADS * VX_DIM, TK), lambda b, i, *_: (b, 0, 0, 0)),
                pl.BlockSpec((1, 1, TQ), lambda b, i, *_: (b, 0, i)),
                _resident((1, seq, 1), lambda b, i, *_: (b, 0, 0)),
                _resident((1, HEAD_DIM), lambda b, i, *_: (0, 0)),
                _resident((1, HEAD_DIM), lambda b, i, *_: (0, 0)),
                _resident((1, HEAD_DIM), lambda b, i, *_: (0, 0)),
                _resident((1, HEAD_DIM), lambda b, i, *_: (0, 0)),
                _resident((V_DIM, 1), lambda b, i, *_: (0, 0)),
                pl.BlockSpec((TQ, ATT_WIDTH), lambda b, i, *_: (b * nq + i, 0)),
            ],
            out_specs=pl.BlockSpec((TQ, ATT_WIDTH), lambda b, i, *_: (b * nq + i, 0)),
            scratch_shapes=[
                pltpu.VMEM((HEADS, V_DIM, 2 * TQ), bf16),
                pltpu.VMEM((HEADS, VX_DIM, 2 * TQ), jnp.float32),
                pltpu.VMEM((HEADS, 1, 2 * TQ), jnp.float32),
                pltpu.VMEM((TK, 2 * TQ), jnp.float32),
                pltpu.VMEM((TK, 2 * TQ), jnp.float32),
                pltpu.VMEM((1, 2 * TQ), jnp.float32),
            ],
        ),
        out_shape=jax.ShapeDtypeStruct((n, ATT_WIDTH), bf16),
        compiler_params=cparams(dimension_semantics=("arbitrary", "arbitrary")),
        name="attn",
    )(klist, n_unm, n_tot, qT, k, vT, pos_row, pos_col,
      lq1.reshape(1, HEAD_DIM), lk1.reshape(1, HEAD_DIM),
      lq2.reshape(1, HEAD_DIM), lk2.reshape(1, HEAD_DIM), subln_w.reshape(V_DIM, 1), sz)

    out = pl.pallas_call(
        _out_kernel,
        grid=(batch, nt),
        in_specs=[
            rows(D_MODEL),
            rows(ATT_WIDTH),
            rows(CONV_WIDTH),
            rows(2 * D_MODEL),
            _resident((ATT_WIDTH, D_MODEL), lambda b, t: (0, 0)),
            _resident((CONV_WIDTH, D_MODEL), lambda b, t: (0, 0)),
            _resident((D_MODEL, D_MODEL), lambda b, t: (0, 0)),
            _resident((1, D_MODEL), lambda b, t: (0, 0)),
        ],
        out_specs=rows(D_MODEL),
        out_shape=jax.ShapeDtypeStruct((n, D_MODEL), jnp.float32),
        compiler_params=cparams(dimension_semantics=("arbitrary", "arbitrary")),
        name="outp",
    )(x2, att, cvs, gates, w_att_out.astype(bf16), w_conv_out.astype(bf16), w_out.astype(bf16),
      post_w.reshape(1, D_MODEL))
    return out


def kernel(x, positions, pre_norm_w, w_in, merge_bias, lambda_q1, lambda_k1, lambda_q2,
           lambda_k2, subln_w, w_att_out, conv_w, w_conv_out, w_out, post_norm_w):
    batch, seq, _ = x.shape
    depth = w_in.shape[0]
    x2 = x.reshape(batch * seq, D_MODEL)
    for layer in range(depth):
        lambda_init = 0.8 - 0.6 * math.exp(-0.3 * layer)
        x2 = _layer(x2, positions, pre_norm_w[layer], w_in[layer], merge_bias[layer],
                    lambda_q1[layer], lambda_k1[layer], lambda_q2[layer], lambda_k2[layer],
                    subln_w[layer], w_att_out[layer], conv_w[layer], w_conv_out[layer],
                    w_out[layer], post_norm_w[layer], lambda_init, batch, seq)
    return x2.reshape(batch, seq, D_MODEL)
```

```python
import functools
import math

import jax
import jax.numpy as jnp
from jax import lax
from jax.experimental import pallas as pl
from jax.experimental.pallas import tpu as pltpu

D_MODEL = 1024
CHUNK_SHIFT = 6
HEADS = 4
HEAD_DIM = 64
V_DIM = 2 * HEAD_DIM
LANES = 128
BF16_ROWS = 16
VX_DIM = V_DIM + BF16_ROWS
QK_WIDTH = 2 * HEADS * HEAD_DIM
ATT_WIDTH = HEADS * V_DIM
CONV_WIDTH = D_MODEL // 2
CONV_K = 3
ROPE_THETA = 500000.0
ROPE_DIM = HEAD_DIM // 4
ROPE_HALF = ROPE_DIM // 2
NORM_EPS = 1e-6
SUBLN_EPS = 1e-5
REST_WIDTH = ATT_WIDTH + 4 * CONV_WIDTH + 2 * D_MODEL
QKV_WIDTH = 2 * QK_WIDTH + ATT_WIDTH
WPREP_COLS = 512

TM = 512
TQ = 512
TK = 512
OUT_ROWS = 256

VMEM_LIMIT = 56 * 1024 * 1024
NEG = float(jnp.finfo(jnp.float32).min) / 2

_NT = (((1,), (1,)), ((), ()))


def _resident(block_shape, index_map):
    return pl.BlockSpec(block_shape, index_map, pipeline_mode=pl.Buffered(1))


def _wprep_kernel(w_ref, wT_ref, wrest_ref, *, n_transposed):
    c = pl.program_id(0)

    @pl.when(c < n_transposed)
    def _():
        wT_ref[...] = w_ref[...].T.astype(jnp.bfloat16)

    @pl.when(c >= n_transposed)
    def _():
        wrest_ref[...] = w_ref[...].astype(jnp.bfloat16)


def _proj_kernel(x_ref, pos_ref, freq_ref, prew_ref, wT_ref, wrest_ref, mb_ref, cw_ref,
                 qT_ref, k_ref, vT_ref, sz_ref, cvs_ref, gate_ref, cu_prev_ref):
    f32 = jnp.float32

    @pl.when(pl.program_id(1) == 0)
    def _():
        cu_prev_ref[...] = jnp.zeros(cu_prev_ref.shape, f32)

    x = x_ref[...]
    ms = jnp.mean(x * x, axis=-1, keepdims=True)
    h = (x * lax.rsqrt(ms + NORM_EPS)) * prew_ref[...]
    hb = h.astype(jnp.bfloat16)

    def sigmoid(v):
        return 0.5 * jnp.tanh(0.5 * v) + 0.5

    def cols(c0, width):
        return jnp.dot(hb, wrest_ref[:, c0:c0 + width], preferred_element_type=f32)

    cw = CONV_WIDTH
    c0 = ATT_WIDTH + 4 * cw
    for c in range(0, 2 * D_MODEL, 512):
        bias = mb_ref[c // D_MODEL:c // D_MODEL + 1, c % D_MODEL:c % D_MODEL + 512]
        gate_ref[:, c:c + 512] = sigmoid(cols(c0 + c, 512) + bias).astype(jnp.bfloat16)

    z_att = cols(0, ATT_WIDTH)
    sz_ref[...] = (z_att * sigmoid(z_att)).astype(jnp.bfloat16)

    c0 = ATT_WIDTH
    gb = cols(c0, cw)
    cu = cols(c0 + cw, cw) * cols(c0 + 2 * cw, cw)
    z_conv = cols(c0 + 3 * cw, cw)

    prev = cu_prev_ref[...]
    h1 = prev[7:8]
    h2 = prev[6:7]
    row = lax.broadcasted_iota(jnp.int32, (8, cw), 0)

    def shifted(by, fix_top):
        r = pltpu.roll(cu, by, 0)
        return jnp.concatenate([fix_top(r[0:8]), r[8:]], axis=0)

    r1 = shifted(1, lambda top: jnp.where(row == 0, h1, top))
    r2 = shifted(2, lambda top: jnp.where(row == 0, h2, jnp.where(row == 1, h1, top)))
    conv = cw_ref[0:1, :] * r2 + cw_ref[1:2, :] * r1 + cw_ref[2:3, :] * cu
    cvs_ref[...] = ((gb * conv) * (z_conv * sigmoid(z_conv))).astype(jnp.bfloat16)
    cu_prev_ref[...] = cu[TM - 8:TM]

    qkT = lax.dot_general(wT_ref[0:2 * QK_WIDTH, :], hb, _NT, preferred_element_type=f32)

    pos = pos_ref[0].astype(jnp.float32)
    ang = freq_ref[...] * pos
    cos = jnp.cos(ang)
    sin = jnp.sin(ang)

    def rot(blk):
        t1 = blk[0:ROPE_HALF]
        t2 = blk[ROPE_HALF:ROPE_DIM]
        return jnp.concatenate(
            [t1 * cos - t2 * sin, t2 * cos + t1 * sin, blk[ROPE_DIM:]], axis=0)

    scale = math.log2(math.e) / math.sqrt(HEAD_DIM)
    for hd in range(2 * HEADS):
        r0 = hd * HEAD_DIM
        qT_ref[0, r0:r0 + HEAD_DIM, :] = (rot(qkT[r0:r0 + HEAD_DIM]) * scale).astype(jnp.bfloat16)
    kT = jnp.concatenate(
        [rot(qkT[QK_WIDTH + hd * HEAD_DIM:QK_WIDTH + (hd + 1) * HEAD_DIM])
         for hd in range(2 * HEADS)], axis=0)
    k = kT.T.astype(jnp.bfloat16)
    for p in range(HEADS):
        k_ref[0, p] = k[:, p * V_DIM:(p + 1) * V_DIM]

    vT = lax.dot_general(wT_ref[2 * QK_WIDTH:QKV_WIDTH, :], hb, _NT, preferred_element_type=f32)
    row = lax.broadcasted_iota(jnp.int32, (BF16_ROWS, TM), 0)
    ones_rows = jnp.where(row == 0, 1.0, 0.0).astype(jnp.bfloat16)
    for p in range(HEADS):
        vT_ref[0, 0, p * VX_DIM:p * VX_DIM + V_DIM, :] = (
            vT[p * V_DIM:(p + 1) * V_DIM].astype(jnp.bfloat16))
        vT_ref[0, 0, p * VX_DIM + V_DIM:(p + 1) * VX_DIM, :] = ones_rows


def _attn_kernel(klist_ref, nunm_ref, ntot_ref,
                 qT_ref, k_ref, vT_ref, posr_ref, posk_ref,
                 lq1_ref, lk1_ref, lq2_ref, lk2_ref, sw_ref, sz_ref,
                 o_ref, q2_ref, acc_ref, m_ref, sa_ref, sb_ref, mt0_ref,
                 *, nq, nk, lambda_init):
    qt = pl.program_id(0) * nq + pl.program_id(1)
    n = ntot_ref[qt]
    n_unm = nunm_ref[qt]

    qc = lax.shift_right_arithmetic(posr_ref[0], CHUNK_SHIFT)
    qc2 = jnp.concatenate([qc, qc], axis=1)

    for p in range(HEADS):
        q2_ref[p] = jnp.zeros(q2_ref.shape[1:], q2_ref.dtype)
        q2_ref[p, 0:HEAD_DIM, 0:TQ] = qT_ref[0, p * V_DIM:p * V_DIM + HEAD_DIM, :]
        q2_ref[p, HEAD_DIM:V_DIM, TQ:2 * TQ] = qT_ref[0, p * V_DIM + HEAD_DIM:(p + 1) * V_DIM, :]
        m_ref[p] = jnp.full(m_ref.shape[1:], NEG, jnp.float32)
        acc_ref[p] = jnp.zeros(acc_ref.shape[1:], jnp.float32)

    def key_tile(r):
        return klist_ref[qt * nk + r]

    def score(p, j, s_ref, masked):
        k0 = pl.multiple_of(j * TK, TK)
        s = jnp.dot(k_ref[0, p, pl.ds(k0, TK), :], q2_ref[p],
                    preferred_element_type=jnp.float32)
        if masked:
            kc = lax.shift_right_arithmetic(posk_ref[0, j], CHUNK_SHIFT)
            kc = jnp.broadcast_to(kc, (LANES, TK)).T
            kc = jnp.concatenate([kc] * (2 * TQ // LANES), axis=1)
            s = jnp.where(kc <= qc2, s, NEG)
        s_ref[...] = s
        return jnp.max(s, axis=0, keepdims=True)

    def update(p, j, s_ref, mt):
        m_old = m_ref[p]
        m_new = jnp.maximum(m_old, mt)
        alpha = jnp.exp2(m_old - m_new)
        pr = jnp.exp2(s_ref[...] - m_new).astype(jnp.bfloat16)
        vt = vT_ref[0, j, p * VX_DIM:(p + 1) * VX_DIM, :]
        acc_ref[p] = alpha * acc_ref[p] + jnp.dot(vt, pr, preferred_element_type=jnp.float32)
        m_ref[p] = m_new

    def block(j, j_next, masked, next_masked):
        mt1 = score(1, j, sb_ref, masked)
        update(0, j, sa_ref, mt0_ref[...])
        mt2 = score(2, j, sa_ref, masked)
        update(1, j, sb_ref, mt1)
        mt3 = score(3, j, sb_ref, masked)
        update(2, j, sa_ref, mt2)
        if next_masked is not None:
            mt0_ref[...] = score(0, j_next, sa_ref, next_masked)
        update(3, j, sb_ref, mt3)

    @pl.when(n_unm > 0)
    def _():
        mt0_ref[...] = score(0, key_tile(0), sa_ref, False)

    @pl.when(n_unm == 0)
    def _():
        mt0_ref[...] = score(0, key_tile(0), sa_ref, True)

    def free_block(r, carry):
        block(key_tile(r), key_tile(r + 1), False, False)
        return carry

    lax.fori_loop(0, n_unm - 1, free_block, 0)

    @pl.when((n_unm > 0) & (n > n_unm))
    def _():
        block(key_tile(n_unm - 1), key_tile(n_unm), False, True)

    @pl.when((n_unm > 0) & (n == n_unm))
    def _():
        block(key_tile(n_unm - 1), None, False, None)

    def masked_block(r, carry):
        block(key_tile(r), key_tile(r + 1), True, True)
        return carry

    lax.fori_loop(n_unm, n - 1, masked_block, 0)

    @pl.when(n > n_unm)
    def _():
        block(key_tile(n - 1), None, True, None)

    lam = (jnp.exp(jnp.sum(lq1_ref[...] * lk1_ref[...], axis=-1, keepdims=True))
           - jnp.exp(jnp.sum(lq2_ref[...] * lk2_ref[...], axis=-1, keepdims=True))
           + lambda_init)
    for p in range(HEADS):
        o = acc_ref[p, 0:V_DIM, :] / acc_ref[p, V_DIM:V_DIM + 1, :]
        d = o[:, :TQ] - lam * o[:, TQ:]
        ms = jnp.mean(d * d, axis=0, keepdims=True)
        dn = ((d * lax.rsqrt(ms + SUBLN_EPS)) * sw_ref[...]) * (1.0 - lambda_init)
        lanes = slice(p * V_DIM, (p + 1) * V_DIM)
        o_ref[:, lanes] = (dn.T * sz_ref[:, lanes].astype(jnp.float32)).astype(o_ref.dtype)


def _out_kernel(x_ref, att_ref, cvs_ref, gate_ref, wao_ref, wco_ref, wo_ref, postw_ref, o_ref):
    f32 = jnp.float32
    for r0 in range(0, TM, OUT_ROWS):
        rows = slice(r0, r0 + OUT_ROWS)
        y_att = jnp.dot(att_ref[rows, :], wao_ref[...], preferred_element_type=f32)
        y_conv = jnp.dot(cvs_ref[rows, :], wco_ref[...], preferred_element_type=f32)
        m = (gate_ref[rows, 0:D_MODEL].astype(f32) * y_att
             + gate_ref[rows, D_MODEL:2 * D_MODEL].astype(f32) * y_conv)
        o = jnp.dot(m.astype(jnp.bfloat16), wo_ref[...], preferred_element_type=f32)
        ms = jnp.mean(o * o, axis=-1, keepdims=True)
        o_ref[rows, :] = x_ref[rows, :] + (o * lax.rsqrt(ms + NORM_EPS)) * postw_ref[...]


def _key_tile_lists(positions, batch, nq, nk):
    chunk = lax.shift_right_arithmetic(positions, CHUNK_SHIFT)
    kch = chunk.reshape(batch, 1, nk, TK)
    qch = chunk.reshape(batch, nq, 1, TQ)
    needed = kch.min(-1) <= qch.max(-1)
    nomask = needed & (kch.max(-1) <= qch.min(-1))
    rank = jnp.where(nomask, 0, jnp.where(needed, 1, 2))
    order = jnp.argsort(rank, axis=-1, stable=True).astype(jnp.int32)

    def count(flag):
        return flag.sum(-1).astype(jnp.int32).reshape(-1)

    return order.reshape(-1), count(nomask), count(needed)


def _layer(x2, positions, pre_w, w_in, merge_bias, lq1, lk1, lq2, lk2, subln_w,
           w_att_out, conv_w, w_conv_out, w_out, post_w, lambda_init, batch, seq):
    bf16 = jnp.bfloat16
    n = batch * seq
    nt = seq // TM
    nq = seq // TQ
    nk = seq // TK
    assert seq % TM == 0 and seq % TQ == 0 and TM == TK

    freqs = (ROPE_THETA ** (-jnp.arange(0, ROPE_DIM, 2, dtype=jnp.float32) / ROPE_DIM)
             ).reshape(ROPE_HALF, 1)
    n_t = QKV_WIDTH // WPREP_COLS
    wT, wrest = pl.pallas_call(
        functools.partial(_wprep_kernel, n_transposed=n_t),
        grid=((QKV_WIDTH + REST_WIDTH) // WPREP_COLS,),
        in_specs=[pl.BlockSpec((D_MODEL, WPREP_COLS), lambda c: (0, c))],
        out_specs=[
            pl.BlockSpec((WPREP_COLS, D_MODEL), lambda c: (jnp.minimum(c, n_t - 1), 0)),
            pl.BlockSpec((D_MODEL, WPREP_COLS), lambda c: (0, jnp.maximum(c - n_t, 0))),
        ],
        out_shape=[
            jax.ShapeDtypeStruct((QKV_WIDTH, D_MODEL), bf16),
            jax.ShapeDtypeStruct((D_MODEL, REST_WIDTH), bf16),
        ],
        compiler_params=pltpu.CompilerParams(dimension_semantics=("arbitrary",)),
        name="wprep",
    )(w_in)
    pos_row = positions.reshape(batch, 1, seq)
    pos_tiles = positions.reshape(batch, nk, 1, TK)

    cparams = functools.partial(pltpu.CompilerParams, vmem_limit_bytes=VMEM_LIMIT)

    def rows(width):
        return pl.BlockSpec((TM, width), lambda b, t: (b * nt + t, 0))

    qT, k, vT, sz, cvs, gates = pl.pallas_call(
        _proj_kernel,
        grid=(batch, nt),
        in_specs=[
            rows(D_MODEL),
            pl.BlockSpec((1, 1, TM), lambda b, t: (b, 0, t)),
            _resident((ROPE_HALF, 1), lambda b, t: (0, 0)),
            _resident((1, D_MODEL), lambda b, t: (0, 0)),
            _resident((QKV_WIDTH, D_MODEL), lambda b, t: (0, 0)),
            _resident((D_MODEL, REST_WIDTH), lambda b, t: (0, 0)),
            _resident((2, D_MODEL), lambda b, t: (0, 0)),
            _resident((CONV_K, CONV_WIDTH), lambda b, t: (0, 0)),
        ],
        out_specs=[
            pl.BlockSpec((1, QK_WIDTH, TM), lambda b, t: (b, 0, t)),
            pl.BlockSpec((1, HEADS, TM, V_DIM), lambda b, t: (b, 0, t, 0)),
            pl.BlockSpec((1, 1, HEADS * VX_DIM, TM), lambda b, t: (b, t, 0, 0)),
            rows(ATT_WIDTH),
            rows(CONV_WIDTH),
            rows(2 * D_MODEL),
        ],
        out_shape=[
            jax.ShapeDtypeStruct((batch, QK_WIDTH, seq), bf16),
            jax.ShapeDtypeStruct((batch, HEADS, seq, V_DIM), bf16),
            jax.ShapeDtypeStruct((batch, nk, HEADS * VX_DIM, TK), bf16),
            jax.ShapeDtypeStruct((n, ATT_WIDTH), bf16),
            jax.ShapeDtypeStruct((n, CONV_WIDTH), bf16),
            jax.ShapeDtypeStruct((n, 2 * D_MODEL), bf16),
        ],
        scratch_shapes=[pltpu.VMEM((8, CONV_WIDTH), jnp.float32)],
        compiler_params=cparams(dimension_semantics=("arbitrary", "arbitrary")),
        name="proj",
    )(x2, pos_row, freqs, pre_w.reshape(1, D_MODEL), wT, wrest,
      merge_bias, conv_w.reshape(CONV_K, CONV_WIDTH))

    klist, n_unm, n_tot = _key_tile_lists(positions, batch, nq, nk)

    att = pl.pallas_call(
        functools.partial(_attn_kernel, nq=nq, nk=nk, lambda_init=lambda_init),
        grid_spec=pltpu.PrefetchScalarGridSpec(
            num_scalar_prefetch=3,
            grid=(batch, nq),
            in_specs=[
                pl.BlockSpec((1, QK_WIDTH, TQ), lambda b, i, *_: (b, 0, i)),
                pl.BlockSpec((1, HEADS, seq, V_DIM), lambda b, i, *_: (b, 0, 0, 0)),
                pl.BlockSpec((1, nk, HEADS * VX_DIM, TK), lambda b, i, *_: (b, 0, 0, 0)),
                pl.BlockSpec((1, 1, TQ), lambda b, i, *_: (b, 0, i)),
                pl.BlockSpec((1, nk, 1, TK), lambda b, i, *_: (b, 0, 0, 0)),
                _resident((1, HEAD_DIM), lambda b, i, *_: (0, 0)),
                _resident((1, HEAD_DIM), lambda b, i, *_: (0, 0)),
                _resident((1, HEAD_DIM), lambda b, i, *_: (0, 0)),
                _resident((1, HEAD_DIM), lambda b, i, *_: (0, 0)),
                _resident((V_DIM, 1), lambda b, i, *_: (0, 0)),
                pl.BlockSpec((TQ, ATT_WIDTH), lambda b, i, *_: (b * nq + i, 0)),
            ],
            out_specs=pl.BlockSpec((TQ, ATT_WIDTH), lambda b, i, *_: (b * nq + i, 0)),
            scratch_shapes=[
                pltpu.VMEM((HEADS, V_DIM, 2 * TQ), bf16),
                pltpu.VMEM((HEADS, VX_DIM, 2 * TQ), jnp.float32),
                pltpu.VMEM((HEADS, 1, 2 * TQ), jnp.float32),
                pltpu.VMEM((TK, 2 * TQ), jnp.float32),
                pltpu.VMEM((TK, 2 * TQ), jnp.float32),
                pltpu.VMEM((1, 2 * TQ), jnp.float32),
            ],
        ),
        out_shape=jax.ShapeDtypeStruct((n, ATT_WIDTH), bf16),
        compiler_params=cparams(dimension_semantics=("arbitrary", "arbitrary")),
        name="attn",
    )(klist, n_unm, n_tot, qT, k, vT, pos_row, pos_tiles,
      lq1.reshape(1, HEAD_DIM), lk1.reshape(1, HEAD_DIM),
      lq2.reshape(1, HEAD_DIM), lk2.reshape(1, HEAD_DIM), subln_w.reshape(V_DIM, 1), sz)

    out = pl.pallas_call(
        _out_kernel,
        grid=(batch, nt),
        in_specs=[
            rows(D_MODEL),
            rows(ATT_WIDTH),
            rows(CONV_WIDTH),
            rows(2 * D_MODEL),
            _resident((ATT_WIDTH, D_MODEL), lambda b, t: (0, 0)),
            _resident((CONV_WIDTH, D_MODEL), lambda b, t: (0, 0)),
            _resident((D_MODEL, D_MODEL), lambda b, t: (0, 0)),
            _resident((1, D_MODEL), lambda b, t: (0, 0)),
        ],
        out_specs=rows(D_MODEL),
        out_shape=jax.ShapeDtypeStruct((n, D_MODEL), jnp.float32),
        compiler_params=cparams(dimension_semantics=("arbitrary", "arbitrary")),
        name="outp",
    )(x2, att, cvs, gates, w_att_out.astype(bf16), w_conv_out.astype(bf16), w_out.astype(bf16),
      post_w.reshape(1, D_MODEL))
    return out


def kernel(x, positions, pre_norm_w, w_in, merge_bias, lambda_q1, lambda_k1, lambda_q2,
           lambda_k2, subln_w, w_att_out, conv_w, w_conv_out, w_out, post_norm_w):
    batch, seq, _ = x.shape
    depth = w_in.shape[0]
    x2 = x.reshape(batch * seq, D_MODEL)
    for layer in range(depth):
        lambda_init = 0.8 - 0.6 * math.exp(-0.3 * layer)
        x2 = _layer(x2, positions, pre_norm_w[layer], w_in[layer], merge_bias[layer],
                    lambda_q1[layer], lambda_k1[layer], lambda_q2[layer], lambda_k2[layer],
                    subln_w[layer], w_att_out[layer], conv_w[layer], w_conv_out[layer],
                    w_out[layer], post_norm_w[layer], lambda_init, batch, seq)
    return x2.reshape(batch, seq, D_MODEL)
```

```python
import functools
import math

import jax
import jax.numpy as jnp
from jax import lax
from jax.experimental import pallas as pl
from jax.experimental.pallas import tpu as pltpu

D_MODEL = 1024
CHUNK_SHIFT = 6
HEADS = 4
HEAD_DIM = 64
V_DIM = 2 * HEAD_DIM
LANES = 128
BF16_ROWS = 16
VX_DIM = V_DIM + BF16_ROWS
QK_WIDTH = 2 * HEADS * HEAD_DIM
ATT_WIDTH = HEADS * V_DIM
CONV_WIDTH = D_MODEL // 2
CONV_K = 3
ROPE_THETA = 500000.0
ROPE_DIM = HEAD_DIM // 4
ROPE_HALF = ROPE_DIM // 2
NORM_EPS = 1e-6
SUBLN_EPS = 1e-5
REST_WIDTH = ATT_WIDTH + 4 * CONV_WIDTH + 2 * D_MODEL
QKV_WIDTH = 2 * QK_WIDTH + ATT_WIDTH
WPREP_COLS = 512

TM = 512
TQ = 512
TK = 512
TMO = 1024
OUT_ROWS = 256

VMEM_LIMIT = 56 * 1024 * 1024
NEG = float(jnp.finfo(jnp.float32).min) / 2

_NT = (((1,), (1,)), ((), ()))


def _resident(block_shape, index_map):
    return pl.BlockSpec(block_shape, index_map, pipeline_mode=pl.Buffered(1))


def _wprep_kernel(w_ref, wT_ref, wrest_ref, *, n_transposed):
    c = pl.program_id(0)

    @pl.when(c < n_transposed)
    def _():
        wT_ref[...] = w_ref[...].T.astype(jnp.bfloat16)

    @pl.when(c >= n_transposed)
    def _():
        wrest_ref[...] = w_ref[...].astype(jnp.bfloat16)


def _proj_kernel(x_ref, pos_ref, freq_ref, prew_ref, wT_ref, wrest_ref, mb_ref, cw_ref,
                 qT_ref, k_ref, vT_ref, sz_ref, cvs_ref, gate_ref, cu_prev_ref):
    f32 = jnp.float32

    @pl.when(pl.program_id(1) == 0)
    def _():
        cu_prev_ref[...] = jnp.zeros(cu_prev_ref.shape, f32)

    x = x_ref[...]
    ms = jnp.mean(x * x, axis=-1, keepdims=True)
    h = (x * lax.rsqrt(ms + NORM_EPS)) * prew_ref[...]
    hb = h.astype(jnp.bfloat16)

    def sigmoid(v):
        return 0.5 * jnp.tanh(0.5 * v) + 0.5

    def cols(c0, width):
        return jnp.dot(hb, wrest_ref[:, c0:c0 + width], preferred_element_type=f32)

    cw = CONV_WIDTH
    c0 = ATT_WIDTH + 4 * cw
    for c in range(0, 2 * D_MODEL, 512):
        bias = mb_ref[c // D_MODEL:c // D_MODEL + 1, c % D_MODEL:c % D_MODEL + 512]
        gate_ref[:, c:c + 512] = sigmoid(cols(c0 + c, 512) + bias).astype(jnp.bfloat16)

    z_att = cols(0, ATT_WIDTH)
    sz_ref[...] = (z_att * sigmoid(z_att)).astype(jnp.bfloat16)

    c0 = ATT_WIDTH
    gb = cols(c0, cw)
    cu = cols(c0 + cw, cw) * cols(c0 + 2 * cw, cw)
    z_conv = cols(c0 + 3 * cw, cw)

    prev = cu_prev_ref[...]
    h1 = prev[7:8]
    h2 = prev[6:7]
    row = lax.broadcasted_iota(jnp.int32, (8, cw), 0)

    def shifted(by, fix_top):
        r = pltpu.roll(cu, by, 0)
        return jnp.concatenate([fix_top(r[0:8]), r[8:]], axis=0)

    r1 = shifted(1, lambda top: jnp.where(row == 0, h1, top))
    r2 = shifted(2, lambda top: jnp.where(row == 0, h2, jnp.where(row == 1, h1, top)))
    conv = cw_ref[0:1, :] * r2 + cw_ref[1:2, :] * r1 + cw_ref[2:3, :] * cu
    cvs_ref[...] = ((gb * conv) * (z_conv * sigmoid(z_conv))).astype(jnp.bfloat16)
    cu_prev_ref[...] = cu[TM - 8:TM]

    qkT = lax.dot_general(wT_ref[0:2 * QK_WIDTH, :], hb, _NT, preferred_element_type=f32)

    pos = pos_ref[0].astype(jnp.float32)
    ang = freq_ref[...] * pos
    cos = jnp.cos(ang)
    sin = jnp.sin(ang)

    def rot(blk):
        t1 = blk[0:ROPE_HALF]
        t2 = blk[ROPE_HALF:ROPE_DIM]
        return jnp.concatenate(
            [t1 * cos - t2 * sin, t2 * cos + t1 * sin, blk[ROPE_DIM:]], axis=0)

    scale = math.log2(math.e) / math.sqrt(HEAD_DIM)
    for hd in range(2 * HEADS):
        r0 = hd * HEAD_DIM
        qT_ref[0, r0:r0 + HEAD_DIM, :] = (rot(qkT[r0:r0 + HEAD_DIM]) * scale).astype(jnp.bfloat16)
    kT = jnp.concatenate(
        [rot(qkT[QK_WIDTH + hd * HEAD_DIM:QK_WIDTH + (hd + 1) * HEAD_DIM])
         for hd in range(2 * HEADS)], axis=0)
    k = kT.T.astype(jnp.bfloat16)
    for p in range(HEADS):
        k_ref[0, p] = k[:, p * V_DIM:(p + 1) * V_DIM]

    vT = lax.dot_general(wT_ref[2 * QK_WIDTH:QKV_WIDTH, :], hb, _NT, preferred_element_type=f32)
    row = lax.broadcasted_iota(jnp.int32, (BF16_ROWS, TM), 0)
    ones_rows = jnp.where(row == 0, 1.0, 0.0).astype(jnp.bfloat16)
    for p in range(HEADS):
        vT_ref[0, 0, p * VX_DIM:p * VX_DIM + V_DIM, :] = (
            vT[p * V_DIM:(p + 1) * V_DIM].astype(jnp.bfloat16))
        vT_ref[0, 0, p * VX_DIM + V_DIM:(p + 1) * VX_DIM, :] = ones_rows


def _attn_kernel(klist_ref, nunm_ref, ntot_ref,
                 qT_ref, k_ref, vT_ref, posr_ref, posk_ref,
                 lq1_ref, lk1_ref, lq2_ref, lk2_ref, sw_ref, sz_ref,
                 o_ref, q2_ref, acc_ref, m_ref, sa_ref, sb_ref, mt0_ref,
                 *, nq, nk, lambda_init):
    qt = pl.program_id(0) * nq + pl.program_id(1)
    n = ntot_ref[qt]
    n_unm = nunm_ref[qt]

    qc = lax.shift_right_arithmetic(posr_ref[0], CHUNK_SHIFT)
    qc2 = jnp.concatenate([qc, qc], axis=1)

    for p in range(HEADS):
        q2_ref[p] = jnp.zeros(q2_ref.shape[1:], q2_ref.dtype)
        q2_ref[p, 0:HEAD_DIM, 0:TQ] = qT_ref[0, p * V_DIM:p * V_DIM + HEAD_DIM, :]
        q2_ref[p, HEAD_DIM:V_DIM, TQ:2 * TQ] = qT_ref[0, p * V_DIM + HEAD_DIM:(p + 1) * V_DIM, :]
        m_ref[p] = jnp.full(m_ref.shape[1:], NEG, jnp.float32)
        acc_ref[p] = jnp.zeros(acc_ref.shape[1:], jnp.float32)

    def key_tile(r):
        return klist_ref[qt * nk + r]

    def score(p, j, s_ref, masked):
        k0 = pl.multiple_of(j * TK, TK)
        s = jnp.dot(k_ref[0, p, pl.ds(k0, TK), :], q2_ref[p],
                    preferred_element_type=jnp.float32)
        if masked:
            kc = lax.shift_right_arithmetic(posk_ref[0, j], CHUNK_SHIFT)
            kc = jnp.broadcast_to(kc, (LANES, TK)).T
            kc = jnp.concatenate([kc] * (2 * TQ // LANES), axis=1)
            s = jnp.where(kc <= qc2, s, NEG)
        s_ref[...] = s
        return jnp.max(s, axis=0, keepdims=True)

    def update(p, j, s_ref, mt):
        m_old = m_ref[p]
        m_new = jnp.maximum(m_old, mt)
        alpha = jnp.exp2(m_old - m_new)
        pr = jnp.exp2(s_ref[...] - m_new).astype(jnp.bfloat16)
        vt = vT_ref[0, j, p * VX_DIM:(p + 1) * VX_DIM, :]
        acc_ref[p] = alpha * acc_ref[p] + jnp.dot(vt, pr, preferred_element_type=jnp.float32)
        m_ref[p] = m_new

    def block(j, j_next, masked, next_masked):
        mt1 = score(1, j, sb_ref, masked)
        update(0, j, sa_ref, mt0_ref[...])
        mt2 = score(2, j, sa_ref, masked)
        update(1, j, sb_ref, mt1)
        mt3 = score(3, j, sb_ref, masked)
        update(2, j, sa_ref, mt2)
        if next_masked is not None:
            mt0_ref[...] = score(0, j_next, sa_ref, next_masked)
        update(3, j, sb_ref, mt3)

    @pl.when(n_unm > 0)
    def _():
        mt0_ref[...] = score(0, key_tile(0), sa_ref, False)

    @pl.when(n_unm == 0)
    def _():
        mt0_ref[...] = score(0, key_tile(0), sa_ref, True)

    def free_block(r, carry):
        block(key_tile(r), key_tile(r + 1), False, False)
        return carry

    lax.fori_loop(0, n_unm - 1, free_block, 0)

    @pl.when((n_unm > 0) & (n > n_unm))
    def _():
        block(key_tile(n_unm - 1), key_tile(n_unm), False, True)

    @pl.when((n_unm > 0) & (n == n_unm))
    def _():
        block(key_tile(n_unm - 1), None, False, None)

    def masked_block(r, carry):
        block(key_tile(r), key_tile(r + 1), True, True)
        return carry

    lax.fori_loop(n_unm, n - 1, masked_block, 0)

    @pl.when(n > n_unm)
    def _():
        block(key_tile(n - 1), None, True, None)

    lam = (jnp.exp(jnp.sum(lq1_ref[...] * lk1_ref[...], axis=-1, keepdims=True))
           - jnp.exp(jnp.sum(lq2_ref[...] * lk2_ref[...], axis=-1, keepdims=True))
           + lambda_init)
    for p in range(HEADS):
        o = acc_ref[p, 0:V_DIM, :] / acc_ref[p, V_DIM:V_DIM + 1, :]
        d = o[:, :TQ] - lam * o[:, TQ:]
        ms = jnp.mean(d * d, axis=0, keepdims=True)
        dn = ((d * lax.rsqrt(ms + SUBLN_EPS)) * sw_ref[...]) * (1.0 - lambda_init)
        lanes = slice(p * V_DIM, (p + 1) * V_DIM)
        o_ref[:, lanes] = (dn.T * sz_ref[:, lanes].astype(jnp.float32)).astype(o_ref.dtype)


def _out_kernel(x_ref, att_ref, cvs_ref, gate_ref, wao_ref, wco_ref, wo_ref, postw_ref, o_ref):
    f32 = jnp.float32
    for r0 in range(0, TMO, OUT_ROWS):
        rows = slice(r0, r0 + OUT_ROWS)
        y_att = jnp.dot(att_ref[rows, :], wao_ref[...], preferred_element_type=f32)
        y_conv = jnp.dot(cvs_ref[rows, :], wco_ref[...], preferred_element_type=f32)
        m = (gate_ref[rows, 0:D_MODEL].astype(f32) * y_att
             + gate_ref[rows, D_MODEL:2 * D_MODEL].astype(f32) * y_conv)
        o = jnp.dot(m.astype(jnp.bfloat16), wo_ref[...], preferred_element_type=f32)
        ms = jnp.mean(o * o, axis=-1, keepdims=True)
        o_ref[rows, :] = x_ref[rows, :] + (o * lax.rsqrt(ms + NORM_EPS)) * postw_ref[...]


def _key_tile_lists(positions, batch, nq, nk):
    chunk = lax.shift_right_arithmetic(positions, CHUNK_SHIFT)
    kch = chunk.reshape(batch, 1, nk, TK)
    qch = chunk.reshape(batch, nq, 1, TQ)
    needed = kch.min(-1) <= qch.max(-1)
    nomask = needed & (kch.max(-1) <= qch.min(-1))
    rank = jnp.where(nomask, 0, jnp.where(needed, 1, 2))
    order = jnp.argsort(rank, axis=-1, stable=True).astype(jnp.int32)

    def count(flag):
        return flag.sum(-1).astype(jnp.int32).reshape(-1)

    return order.reshape(-1), count(nomask), count(needed)


def _layer(x2, positions, pre_w, w_in, merge_bias, lq1, lk1, lq2, lk2, subln_w,
           w_att_out, conv_w, w_conv_out, w_out, post_w, lambda_init, batch, seq):
    bf16 = jnp.bfloat16
    n = batch * seq
    nt = seq // TM
    nq = seq // TQ
    nk = seq // TK
    assert seq % TM == 0 and seq % TQ == 0 and TM == TK

    freqs = (ROPE_THETA ** (-jnp.arange(0, ROPE_DIM, 2, dtype=jnp.float32) / ROPE_DIM)
             ).reshape(ROPE_HALF, 1)
    n_t = QKV_WIDTH // WPREP_COLS
    wT, wrest = pl.pallas_call(
        functools.partial(_wprep_kernel, n_transposed=n_t),
        grid=((QKV_WIDTH + REST_WIDTH) // WPREP_COLS,),
        in_specs=[pl.BlockSpec((D_MODEL, WPREP_COLS), lambda c: (0, c))],
        out_specs=[
            pl.BlockSpec((WPREP_COLS, D_MODEL), lambda c: (jnp.minimum(c, n_t - 1), 0)),
            pl.BlockSpec((D_MODEL, WPREP_COLS), lambda c: (0, jnp.maximum(c - n_t, 0))),
        ],
        out_shape=[
            jax.ShapeDtypeStruct((QKV_WIDTH, D_MODEL), bf16),
            jax.ShapeDtypeStruct((D_MODEL, REST_WIDTH), bf16),
        ],
        compiler_params=pltpu.CompilerParams(dimension_semantics=("arbitrary",)),
        name="wprep",
    )(w_in)
    pos_row = positions.reshape(batch, 1, seq)
    pos_tiles = positions.reshape(batch, nk, 1, TK)

    cparams = functools.partial(pltpu.CompilerParams, vmem_limit_bytes=VMEM_LIMIT)

    def rows(width):
        return pl.BlockSpec((TM, width), lambda b, t: (b * nt + t, 0))

    qT, k, vT, sz, cvs, gates = pl.pallas_call(
        _proj_kernel,
        grid=(batch, nt),
        in_specs=[
            rows(D_MODEL),
            pl.BlockSpec((1, 1, TM), lambda b, t: (b, 0, t)),
            _resident((ROPE_HALF, 1), lambda b, t: (0, 0)),
            _resident((1, D_MODEL), lambda b, t: (0, 0)),
            _resident((QKV_WIDTH, D_MODEL), lambda b, t: (0, 0)),
            _resident((D_MODEL, REST_WIDTH), lambda b, t: (0, 0)),
            _resident((2, D_MODEL), lambda b, t: (0, 0)),
            _resident((CONV_K, CONV_WIDTH), lambda b, t: (0, 0)),
        ],
        out_specs=[
            pl.BlockSpec((1, QK_WIDTH, TM), lambda b, t: (b, 0, t)),
            pl.BlockSpec((1, HEADS, TM, V_DIM), lambda b, t: (b, 0, t, 0)),
            pl.BlockSpec((1, 1, HEADS * VX_DIM, TM), lambda b, t: (b, t, 0, 0)),
            rows(ATT_WIDTH),
            rows(CONV_WIDTH),
            rows(2 * D_MODEL),
        ],
        out_shape=[
            jax.ShapeDtypeStruct((batch, QK_WIDTH, seq), bf16),
            jax.ShapeDtypeStruct((batch, HEADS, seq, V_DIM), bf16),
            jax.ShapeDtypeStruct((batch, nk, HEADS * VX_DIM, TK), bf16),
            jax.ShapeDtypeStruct((n, ATT_WIDTH), bf16),
            jax.ShapeDtypeStruct((n, CONV_WIDTH), bf16),
            jax.ShapeDtypeStruct((n, 2 * D_MODEL), bf16),
        ],
        scratch_shapes=[pltpu.VMEM((8, CONV_WIDTH), jnp.float32)],
        compiler_params=cparams(dimension_semantics=("arbitrary", "arbitrary")),
        name="proj",
    )(x2, pos_row, freqs, pre_w.reshape(1, D_MODEL), wT, wrest,
      merge_bias, conv_w.reshape(CONV_K, CONV_WIDTH))

    klist, n_unm, n_tot = _key_tile_lists(positions, batch, nq, nk)

    att = pl.pallas_call(
        functools.partial(_attn_kernel, nq=nq, nk=nk, lambda_init=lambda_init),
        grid_spec=pltpu.PrefetchScalarGridSpec(
            num_scalar_prefetch=3,
            grid=(batch, nq),
            in_specs=[
                pl.BlockSpec((1, QK_WIDTH, TQ), lambda b, i, *_: (b, 0, i)),
                pl.BlockSpec((1, HEADS, seq, V_DIM), lambda b, i, *_: (b, 0, 0, 0)),
                pl.BlockSpec((1, nk, HEADS * VX_DIM, TK), lambda b, i, *_: (b, 0, 0, 0)),
                pl.BlockSpec((1, 1, TQ), lambda b, i, *_: (b, 0, i)),
                pl.BlockSpec((1, nk, 1, TK), lambda b, i, *_: (b, 0, 0, 0)),
                _resident((1, HEAD_DIM), lambda b, i, *_: (0, 0)),
                _resident((1, HEAD_DIM), lambda b, i, *_: (0, 0)),
                _resident((1, HEAD_DIM), lambda b, i, *_: (0, 0)),
                _resident((1, HEAD_DIM), lambda b, i, *_: (0, 0)),
                _resident((V_DIM, 1), lambda b, i, *_: (0, 0)),
                pl.BlockSpec((TQ, ATT_WIDTH), lambda b, i, *_: (b * nq + i, 0)),
            ],
            out_specs=pl.BlockSpec((TQ, ATT_WIDTH), lambda b, i, *_: (b * nq + i, 0)),
            scratch_shapes=[
                pltpu.VMEM((HEADS, V_DIM, 2 * TQ), bf16),
                pltpu.VMEM((HEADS, VX_DIM, 2 * TQ), jnp.float32),
                pltpu.VMEM((HEADS, 1, 2 * TQ), jnp.float32),
                pltpu.VMEM((TK, 2 * TQ), jnp.float32),
                pltpu.VMEM((TK, 2 * TQ), jnp.float32),
                pltpu.VMEM((1, 2 * TQ), jnp.float32),
            ],
        ),
        out_shape=jax.ShapeDtypeStruct((n, ATT_WIDTH), bf16),
        compiler_params=cparams(dimension_semantics=("arbitrary", "arbitrary")),
        name="attn",
    )(klist, n_unm, n_tot, qT, k, vT, pos_row, pos_tiles,
      lq1.reshape(1, HEAD_DIM), lk1.reshape(1, HEAD_DIM),
      lq2.reshape(1, HEAD_DIM), lk2.reshape(1, HEAD_DIM), subln_w.reshape(V_DIM, 1), sz)

    out = pl.pallas_call(
        _out_kernel,
        grid=(n // TMO,),
        in_specs=[
            pl.BlockSpec((TMO, D_MODEL), lambda t: (t, 0)),
            pl.BlockSpec((TMO, ATT_WIDTH), lambda t: (t, 0)),
            pl.BlockSpec((TMO, CONV_WIDTH), lambda t: (t, 0)),
            pl.BlockSpec((TMO, 2 * D_MODEL), lambda t: (t, 0)),
            _resident((ATT_WIDTH, D_MODEL), lambda t: (0, 0)),
            _resident((CONV_WIDTH, D_MODEL), lambda t: (0, 0)),
            _resident((D_MODEL, D_MODEL), lambda t: (0, 0)),
            _resident((1, D_MODEL), lambda t: (0, 0)),
        ],
        out_specs=pl.BlockSpec((TMO, D_MODEL), lambda t: (t, 0)),
        out_shape=jax.ShapeDtypeStruct((n, D_MODEL), jnp.float32),
        compiler_params=cparams(dimension_semantics=("arbitrary",)),
        name="outp",
    )(x2, att, cvs, gates, w_att_out.astype(bf16), w_conv_out.astype(bf16), w_out.astype(bf16),
      post_w.reshape(1, D_MODEL))
    return out


def kernel(x, positions, pre_norm_w, w_in, merge_bias, lambda_q1, lambda_k1, lambda_q2,
           lambda_k2, subln_w, w_att_out, conv_w, w_conv_out, w_out, post_norm_w):
    batch, seq, _ = x.shape
    depth = w_in.shape[0]
    x2 = x.reshape(batch * seq, D_MODEL)
    for layer in range(depth):
        lambda_init = 0.8 - 0.6 * math.exp(-0.3 * layer)
        x2 = _layer(x2, positions, pre_norm_w[layer], w_in[layer], merge_bias[layer],
                    lambda_q1[layer], lambda_k1[layer], lambda_q2[layer], lambda_k2[layer],
                    subln_w[layer], w_att_out[layer], conv_w[layer], w_conv_out[layer],
                    w_out[layer], post_norm_w[layer], lambda_init, batch, seq)
    return x2.reshape(batch, seq, D_MODEL)
```

```python
import functools
import math

import jax
import jax.numpy as jnp
from jax import lax
from jax.experimental import pallas as pl
from jax.experimental.pallas import tpu as pltpu

D_MODEL = 1024
CHUNK_SHIFT = 6
HEADS = 4
HEAD_DIM = 64
V_DIM = 2 * HEAD_DIM
LANES = 128
BF16_ROWS = 16
VX_DIM = V_DIM + BF16_ROWS
QK_WIDTH = 2 * HEADS * HEAD_DIM
ATT_WIDTH = HEADS * V_DIM
CONV_WIDTH = D_MODEL // 2
CONV_K = 3
ROPE_THETA = 500000.0
ROPE_DIM = HEAD_DIM // 4
ROPE_HALF = ROPE_DIM // 2
NORM_EPS = 1e-6
SUBLN_EPS = 1e-5
REST_WIDTH = ATT_WIDTH + 4 * CONV_WIDTH + 2 * D_MODEL
QKV_WIDTH = 2 * QK_WIDTH + ATT_WIDTH
WPREP_COLS = 512

TM = 512
TMP = 1024
TQ = 512
TK = 512
TMO = 1024
OUT_ROWS = 256

VMEM_LIMIT = 56 * 1024 * 1024
NEG = float(jnp.finfo(jnp.float32).min) / 2

_NT = (((1,), (1,)), ((), ()))


def _resident(block_shape, index_map):
    return pl.BlockSpec(block_shape, index_map, pipeline_mode=pl.Buffered(1))


def _wprep_kernel(w_ref, wT_ref, wrest_ref, *, n_transposed):
    c = pl.program_id(0)

    @pl.when(c < n_transposed)
    def _():
        wT_ref[...] = w_ref[...].T.astype(jnp.bfloat16)

    @pl.when(c >= n_transposed)
    def _():
        wrest_ref[...] = w_ref[...].astype(jnp.bfloat16)


def _proj_kernel(x_ref, pos_ref, freq_ref, prew_ref, wT_ref, wrest_ref, mb_ref, cw_ref,
                 qT_ref, k_ref, vT_ref, sz_ref, cvs_ref, gate_ref, cu_prev_ref):
    f32 = jnp.float32
    cw = CONV_WIDTH

    @pl.when(pl.program_id(1) == 0)
    def _():
        cu_prev_ref[...] = jnp.zeros(cu_prev_ref.shape, f32)

    def sigmoid(v):
        return 0.5 * jnp.tanh(0.5 * v) + 0.5

    def chain(ci, prev):
        rows = slice(ci * TM, (ci + 1) * TM)
        x = x_ref[rows, :]
        ms = jnp.mean(x * x, axis=-1, keepdims=True)
        h = (x * lax.rsqrt(ms + NORM_EPS)) * prew_ref[...]
        hb = h.astype(jnp.bfloat16)

        def cols(c0, width):
            return jnp.dot(hb, wrest_ref[:, c0:c0 + width], preferred_element_type=f32)

        c0 = ATT_WIDTH + 4 * cw
        for c in range(0, 2 * D_MODEL, 512):
            bias = mb_ref[c // D_MODEL:c // D_MODEL + 1, c % D_MODEL:c % D_MODEL + 512]
            gate_ref[rows, c:c + 512] = sigmoid(cols(c0 + c, 512) + bias).astype(jnp.bfloat16)

        z_att = cols(0, ATT_WIDTH)
        sz_ref[rows, :] = (z_att * sigmoid(z_att)).astype(jnp.bfloat16)

        c0 = ATT_WIDTH
        gb = cols(c0, cw)
        cu = cols(c0 + cw, cw) * cols(c0 + 2 * cw, cw)
        z_conv = cols(c0 + 3 * cw, cw)

        h1 = prev[7:8]
        h2 = prev[6:7]
        row = lax.broadcasted_iota(jnp.int32, (8, cw), 0)

        def shifted(by, fix_top):
            r = pltpu.roll(cu, by, 0)
            return jnp.concatenate([fix_top(r[0:8]), r[8:]], axis=0)

        r1 = shifted(1, lambda top: jnp.where(row == 0, h1, top))
        r2 = shifted(2, lambda top: jnp.where(row == 0, h2, jnp.where(row == 1, h1, top)))
        conv = cw_ref[0:1, :] * r2 + cw_ref[1:2, :] * r1 + cw_ref[2:3, :] * cu
        cvs_ref[rows, :] = ((gb * conv) * (z_conv * sigmoid(z_conv))).astype(jnp.bfloat16)

        qkT = lax.dot_general(wT_ref[0:2 * QK_WIDTH, :], hb, _NT, preferred_element_type=f32)

        pos = pos_ref[0, :, rows].astype(jnp.float32)
        ang = freq_ref[...] * pos
        cos = jnp.cos(ang)
        sin = jnp.sin(ang)

        def rot(blk):
            t1 = blk[0:ROPE_HALF]
            t2 = blk[ROPE_HALF:ROPE_DIM]
            return jnp.concatenate(
                [t1 * cos - t2 * sin, t2 * cos + t1 * sin, blk[ROPE_DIM:]], axis=0)

        scale = math.log2(math.e) / math.sqrt(HEAD_DIM)
        for hd in range(2 * HEADS):
            r0 = hd * HEAD_DIM
            qT_ref[0, r0:r0 + HEAD_DIM, rows] = (
                rot(qkT[r0:r0 + HEAD_DIM]) * scale).astype(jnp.bfloat16)
        kT = jnp.concatenate(
            [rot(qkT[QK_WIDTH + hd * HEAD_DIM:QK_WIDTH + (hd + 1) * HEAD_DIM])
             for hd in range(2 * HEADS)], axis=0)
        k = kT.T.astype(jnp.bfloat16)
        for p in range(HEADS):
            k_ref[0, p, rows, :] = k[:, p * V_DIM:(p + 1) * V_DIM]

        vT = lax.dot_general(wT_ref[2 * QK_WIDTH:QKV_WIDTH, :], hb, _NT,
                             preferred_element_type=f32)
        one_row = lax.broadcasted_iota(jnp.int32, (BF16_ROWS, TM), 0) == 0
        ones_rows = jnp.where(one_row, 1.0, 0.0).astype(jnp.bfloat16)
        for p in range(HEADS):
            vT_ref[0, ci, p * VX_DIM:p * VX_DIM + V_DIM, :] = (
                vT[p * V_DIM:(p + 1) * V_DIM].astype(jnp.bfloat16))
            vT_ref[0, ci, p * VX_DIM + V_DIM:(p + 1) * VX_DIM, :] = ones_rows
        return cu[TM - 8:TM]

    prev = cu_prev_ref[...]
    for ci in range(TMP // TM):
        prev = chain(ci, prev)
    cu_prev_ref[...] = prev


def _attn_kernel(klist_ref, nunm_ref, ntot_ref,
                 qT_ref, k_ref, vT_ref, posr_ref, posk_ref,
                 lq1_ref, lk1_ref, lq2_ref, lk2_ref, sw_ref, sz_ref,
                 o_ref, q2_ref, acc_ref, m_ref, sa_ref, sb_ref, mt0_ref,
                 *, nq, nk, lambda_init):
    qt = pl.program_id(0) * nq + pl.program_id(1)
    n = ntot_ref[qt]
    n_unm = nunm_ref[qt]

    qc = lax.shift_right_arithmetic(posr_ref[0], CHUNK_SHIFT)
    qc2 = jnp.concatenate([qc, qc], axis=1)

    for p in range(HEADS):
        q2_ref[p] = jnp.zeros(q2_ref.shape[1:], q2_ref.dtype)
        q2_ref[p, 0:HEAD_DIM, 0:TQ] = qT_ref[0, p * V_DIM:p * V_DIM + HEAD_DIM, :]
        q2_ref[p, HEAD_DIM:V_DIM, TQ:2 * TQ] = qT_ref[0, p * V_DIM + HEAD_DIM:(p + 1) * V_DIM, :]
        m_ref[p] = jnp.full(m_ref.shape[1:], NEG, jnp.float32)
        acc_ref[p] = jnp.zeros(acc_ref.shape[1:], jnp.float32)

    def key_tile(r):
        return klist_ref[qt * nk + r]

    def score(p, j, s_ref, masked):
        k0 = pl.multiple_of(j * TK, TK)
        s = jnp.dot(k_ref[0, p, pl.ds(k0, TK), :], q2_ref[p],
                    preferred_element_type=jnp.float32)
        if masked:
            kc = lax.shift_right_arithmetic(posk_ref[0, j], CHUNK_SHIFT)
            kc = jnp.broadcast_to(kc, (LANES, TK)).T
            kc = jnp.concatenate([kc] * (2 * TQ // LANES), axis=1)
            s = jnp.where(kc <= qc2, s, NEG)
        s_ref[...] = s
        return jnp.max(s, axis=0, keepdims=True)

    def update(p, j, s_ref, mt):
        m_old = m_ref[p]
        m_new = jnp.maximum(m_old, mt)
        alpha = jnp.exp2(m_old - m_new)
        pr = jnp.exp2(s_ref[...] - m_new).astype(jnp.bfloat16)
        vt = vT_ref[0, j, p * VX_DIM:(p + 1) * VX_DIM, :]
        acc_ref[p] = alpha * acc_ref[p] + jnp.dot(vt, pr, preferred_element_type=jnp.float32)
        m_ref[p] = m_new

    def block(j, j_next, masked, next_masked):
        mt1 = score(1, j, sb_ref, masked)
        update(0, j, sa_ref, mt0_ref[...])
        mt2 = score(2, j, sa_ref, masked)
        update(1, j, sb_ref, mt1)
        mt3 = score(3, j, sb_ref, masked)
        update(2, j, sa_ref, mt2)
        if next_masked is not None:
            mt0_ref[...] = score(0, j_next, sa_ref, next_masked)
        update(3, j, sb_ref, mt3)

    @pl.when(n_unm > 0)
    def _():
        mt0_ref[...] = score(0, key_tile(0), sa_ref, False)

    @pl.when(n_unm == 0)
    def _():
        mt0_ref[...] = score(0, key_tile(0), sa_ref, True)

    def free_block(r, carry):
        block(key_tile(r), key_tile(r + 1), False, False)
        return carry

    lax.fori_loop(0, n_unm - 1, free_block, 0)

    @pl.when((n_unm > 0) & (n > n_unm))
    def _():
        block(key_tile(n_unm - 1), key_tile(n_unm), False, True)

    @pl.when((n_unm > 0) & (n == n_unm))
    def _():
        block(key_tile(n_unm - 1), None, False, None)

    def masked_block(r, carry):
        block(key_tile(r), key_tile(r + 1), True, True)
        return carry

    lax.fori_loop(n_unm, n - 1, masked_block, 0)

    @pl.when(n > n_unm)
    def _():
        block(key_tile(n - 1), None, True, None)

    lam = (jnp.exp(jnp.sum(lq1_ref[...] * lk1_ref[...], axis=-1, keepdims=True))
           - jnp.exp(jnp.sum(lq2_ref[...] * lk2_ref[...], axis=-1, keepdims=True))
           + lambda_init)
    for p in range(HEADS):
        o = acc_ref[p, 0:V_DIM, :] / acc_ref[p, V_DIM:V_DIM + 1, :]
        d = o[:, :TQ] - lam * o[:, TQ:]
        ms = jnp.mean(d * d, axis=0, keepdims=True)
        dn = ((d * lax.rsqrt(ms + SUBLN_EPS)) * sw_ref[...]) * (1.0 - lambda_init)
        lanes = slice(p * V_DIM, (p + 1) * V_DIM)
        o_ref[:, lanes] = (dn.T * sz_ref[:, lanes].astype(jnp.float32)).astype(o_ref.dtype)


def _out_kernel(x_ref, att_ref, cvs_ref, gate_ref, wao_ref, wco_ref, wo_ref, postw_ref, o_ref):
    f32 = jnp.float32
    for r0 in range(0, TMO, OUT_ROWS):
        rows = slice(r0, r0 + OUT_ROWS)
        y_att = jnp.dot(att_ref[rows, :], wao_ref[...], preferred_element_type=f32)
        y_conv = jnp.dot(cvs_ref[rows, :], wco_ref[...], preferred_element_type=f32)
        m = (gate_ref[rows, 0:D_MODEL].astype(f32) * y_att
             + gate_ref[rows, D_MODEL:2 * D_MODEL].astype(f32) * y_conv)
        o = jnp.dot(m.astype(jnp.bfloat16), wo_ref[...], preferred_element_type=f32)
        ms = jnp.mean(o * o, axis=-1, keepdims=True)
        o_ref[rows, :] = x_ref[rows, :] + (o * lax.rsqrt(ms + NORM_EPS)) * postw_ref[...]


def _key_tile_lists(positions, batch, nq, nk):
    chunk = lax.shift_right_arithmetic(positions, CHUNK_SHIFT)
    kch = chunk.reshape(batch, 1, nk, TK)
    qch = chunk.reshape(batch, nq, 1, TQ)
    needed = kch.min(-1) <= qch.max(-1)
    nomask = needed & (kch.max(-1) <= qch.min(-1))
    rank = jnp.where(nomask, 0, jnp.where(needed, 1, 2))
    order = jnp.argsort(rank, axis=-1, stable=True).astype(jnp.int32)

    def count(flag):
        return flag.sum(-1).astype(jnp.int32).reshape(-1)

    return order.reshape(-1), count(nomask), count(needed)


def _layer(x2, positions, pre_w, w_in, merge_bias, lq1, lk1, lq2, lk2, subln_w,
           w_att_out, conv_w, w_conv_out, w_out, post_w, lambda_init, batch, seq):
    bf16 = jnp.bfloat16
    n = batch * seq
    nt = seq // TMP
    nq = seq // TQ
    nk = seq // TK
    assert seq % TMP == 0 and seq % TQ == 0 and TM == TK and n % TMO == 0

    freqs = (ROPE_THETA ** (-jnp.arange(0, ROPE_DIM, 2, dtype=jnp.float32) / ROPE_DIM)
             ).reshape(ROPE_HALF, 1)
    n_t = QKV_WIDTH // WPREP_COLS
    wT, wrest = pl.pallas_call(
        functools.partial(_wprep_kernel, n_transposed=n_t),
        grid=((QKV_WIDTH + REST_WIDTH) // WPREP_COLS,),
        in_specs=[pl.BlockSpec((D_MODEL, WPREP_COLS), lambda c: (0, c))],
        out_specs=[
            pl.BlockSpec((WPREP_COLS, D_MODEL), lambda c: (jnp.minimum(c, n_t - 1), 0)),
            pl.BlockSpec((D_MODEL, WPREP_COLS), lambda c: (0, jnp.maximum(c - n_t, 0))),
        ],
        out_shape=[
            jax.ShapeDtypeStruct((QKV_WIDTH, D_MODEL), bf16),
            jax.ShapeDtypeStruct((D_MODEL, REST_WIDTH), bf16),
        ],
        compiler_params=pltpu.CompilerParams(dimension_semantics=("arbitrary",)),
        name="wprep",
    )(w_in)
    pos_row = positions.reshape(batch, 1, seq)
    pos_tiles = positions.reshape(batch, nk, 1, TK)

    cparams = functools.partial(pltpu.CompilerParams, vmem_limit_bytes=VMEM_LIMIT)

    def rows(width):
        return pl.BlockSpec((TMP, width), lambda b, t: (b * nt + t, 0))

    qT, k, vT, sz, cvs, gates = pl.pallas_call(
        _proj_kernel,
        grid=(batch, nt),
        in_specs=[
            rows(D_MODEL),
            pl.BlockSpec((1, 1, TMP), lambda b, t: (b, 0, t)),
            _resident((ROPE_HALF, 1), lambda b, t: (0, 0)),
            _resident((1, D_MODEL), lambda b, t: (0, 0)),
            _resident((QKV_WIDTH, D_MODEL), lambda b, t: (0, 0)),
            _resident((D_MODEL, REST_WIDTH), lambda b, t: (0, 0)),
            _resident((2, D_MODEL), lambda b, t: (0, 0)),
            _resident((CONV_K, CONV_WIDTH), lambda b, t: (0, 0)),
        ],
        out_specs=[
            pl.BlockSpec((1, QK_WIDTH, TMP), lambda b, t: (b, 0, t)),
            pl.BlockSpec((1, HEADS, TMP, V_DIM), lambda b, t: (b, 0, t, 0)),
            pl.BlockSpec((1, TMP // TK, HEADS * VX_DIM, TK), lambda b, t: (b, t, 0, 0)),
            rows(ATT_WIDTH),
            rows(CONV_WIDTH),
            rows(2 * D_MODEL),
        ],
        out_shape=[
            jax.ShapeDtypeStruct((batch, QK_WIDTH, seq), bf16),
            jax.ShapeDtypeStruct((batch, HEADS, seq, V_DIM), bf16),
            jax.ShapeDtypeStruct((batch, nk, HEADS * VX_DIM, TK), bf16),
            jax.ShapeDtypeStruct((n, ATT_WIDTH), bf16),
            jax.ShapeDtypeStruct((n, CONV_WIDTH), bf16),
            jax.ShapeDtypeStruct((n, 2 * D_MODEL), bf16),
        ],
        scratch_shapes=[pltpu.VMEM((8, CONV_WIDTH), jnp.float32)],
        compiler_params=cparams(dimension_semantics=("arbitrary", "arbitrary")),
        name="proj",
    )(x2, pos_row, freqs, pre_w.reshape(1, D_MODEL), wT, wrest,
      merge_bias, conv_w.reshape(CONV_K, CONV_WIDTH))

    klist, n_unm, n_tot = _key_tile_lists(positions, batch, nq, nk)

    att = pl.pallas_call(
        functools.partial(_attn_kernel, nq=nq, nk=nk, lambda_init=lambda_init),
        grid_spec=pltpu.PrefetchScalarGridSpec(
            num_scalar_prefetch=3,
            grid=(batch, nq),
            in_specs=[
                pl.BlockSpec((1, QK_WIDTH, TQ), lambda b, i, *_: (b, 0, i)),
                pl.BlockSpec((1, HEADS, seq, V_DIM), lambda b, i, *_: (b, 0, 0, 0)),
                pl.BlockSpec((1, nk, HEADS * VX_DIM, TK), lambda b, i, *_: (b, 0, 0, 0)),
                pl.BlockSpec((1, 1, TQ), lambda b, i, *_: (b, 0, i)),
                pl.BlockSpec((1, nk, 1, TK), lambda b, i, *_: (b, 0, 0, 0)),
                _resident((1, HEAD_DIM), lambda b, i, *_: (0, 0)),
                _resident((1, HEAD_DIM), lambda b, i, *_: (0, 0)),
                _resident((1, HEAD_DIM), lambda b, i, *_: (0, 0)),
                _resident((1, HEAD_DIM), lambda b, i, *_: (0, 0)),
                _resident((V_DIM, 1), lambda b, i, *_: (0, 0)),
                pl.BlockSpec((TQ, ATT_WIDTH), lambda b, i, *_: (b * nq + i, 0)),
            ],
            out_specs=pl.BlockSpec((TQ, ATT_WIDTH), lambda b, i, *_: (b * nq + i, 0)),
            scratch_shapes=[
                pltpu.VMEM((HEADS, V_DIM, 2 * TQ), bf16),
                pltpu.VMEM((HEADS, VX_DIM, 2 * TQ), jnp.float32),
                pltpu.VMEM((HEADS, 1, 2 * TQ), jnp.float32),
                pltpu.VMEM((TK, 2 * TQ), jnp.float32),
                pltpu.VMEM((TK, 2 * TQ), jnp.float32),
                pltpu.VMEM((1, 2 * TQ), jnp.float32),
            ],
        ),
        out_shape=jax.ShapeDtypeStruct((n, ATT_WIDTH), bf16),
        compiler_params=cparams(dimension_semantics=("arbitrary", "arbitrary")),
        name="attn",
    )(klist, n_unm, n_tot, qT, k, vT, pos_row, pos_tiles,
      lq1.reshape(1, HEAD_DIM), lk1.reshape(1, HEAD_DIM),
      lq2.reshape(1, HEAD_DIM), lk2.reshape(1, HEAD_DIM), subln_w.reshape(V_DIM, 1), sz)

    out = pl.pallas_call(
        _out_kernel,
        grid=(n // TMO,),
        in_specs=[
            pl.BlockSpec((TMO, D_MODEL), lambda t: (t, 0)),
            pl.BlockSpec((TMO, ATT_WIDTH), lambda t: (t, 0)),
            pl.BlockSpec((TMO, CONV_WIDTH), lambda t: (t, 0)),
            pl.BlockSpec((TMO, 2 * D_MODEL), lambda t: (t, 0)),
            _resident((ATT_WIDTH, D_MODEL), lambda t: (0, 0)),
            _resident((CONV_WIDTH, D_MODEL), lambda t: (0, 0)),
            _resident((D_MODEL, D_MODEL), lambda t: (0, 0)),
            _resident((1, D_MODEL), lambda t: (0, 0)),
        ],
        out_specs=pl.BlockSpec((TMO, D_MODEL), lambda t: (t, 0)),
        out_shape=jax.ShapeDtypeStruct((n, D_MODEL), jnp.float32),
        compiler_params=cparams(dimension_semantics=("arbitrary",)),
        name="outp",
    )(x2, att, cvs, gates, w_att_out.astype(bf16), w_conv_out.astype(bf16), w_out.astype(bf16),
      post_w.reshape(1, D_MODEL))
    return out


def kernel(x, positions, pre_norm_w, w_in, merge_bias, lambda_q1, lambda_k1, lambda_q2,
           lambda_k2, subln_w, w_att_out, conv_w, w_conv_out, w_out, post_norm_w):
    batch, seq, _ = x.shape
    depth = w_in.shape[0]
    x2 = x.reshape(batch * seq, D_MODEL)
    for layer in range(depth):
        lambda_init = 0.8 - 0.6 * math.exp(-0.3 * layer)
        x2 = _layer(x2, positions, pre_norm_w[layer], w_in[layer], merge_bias[layer],
                    lambda_q1[layer], lambda_k1[layer], lambda_q2[layer], lambda_k2[layer],
                    subln_w[layer], w_att_out[layer], conv_w[layer], w_conv_out[layer],
                    w_out[layer], post_norm_w[layer], lambda_init, batch, seq)
    return x2.reshape(batch, seq, D_MODEL)
```

```python
import functools
import math

import jax
import jax.numpy as jnp
from jax import lax
from jax.experimental import pallas as pl
from jax.experimental.pallas import tpu as pltpu

D_MODEL = 1024
CHUNK_SHIFT = 6
HEADS = 4
HEAD_DIM = 64
V_DIM = 2 * HEAD_DIM
LANES = 128
BF16_ROWS = 16
VX_DIM = V_DIM + BF16_ROWS
QK_WIDTH = 2 * HEADS * HEAD_DIM
ATT_WIDTH = HEADS * V_DIM
CONV_WIDTH = D_MODEL // 2
CONV_K = 3
ROPE_THETA = 500000.0
ROPE_DIM = HEAD_DIM // 4
ROPE_HALF = ROPE_DIM // 2
NORM_EPS = 1e-6
SUBLN_EPS = 1e-5
REST_WIDTH = ATT_WIDTH + 4 * CONV_WIDTH + 2 * D_MODEL
QKV_WIDTH = 2 * QK_WIDTH + ATT_WIDTH
WPREP_COLS = 512

TM = 512
TMP = 1024
TQ = 512
TK = 512
TMO = 1024
OUT_ROWS = 256

VMEM_LIMIT = 56 * 1024 * 1024
NEG = float(jnp.finfo(jnp.float32).min) / 2

_NT = (((1,), (1,)), ((), ()))


def _resident(block_shape, index_map):
    return pl.BlockSpec(block_shape, index_map, pipeline_mode=pl.Buffered(1))


def _wprep_kernel(w_ref, wT_ref, wrest_ref, *, n_transposed):
    c = pl.program_id(0)

    @pl.when(c < n_transposed)
    def _():
        wT_ref[...] = w_ref[...].T.astype(jnp.bfloat16)

    @pl.when(c >= n_transposed)
    def _():
        wrest_ref[...] = w_ref[...].astype(jnp.bfloat16)


def _proj_kernel(x_ref, pos_ref, freq_ref, prew_ref, wT_ref, wrest_ref, mb_ref, cw_ref,
                 qT_ref, k_ref, vT_ref, sz_ref, cvs_ref, gate_ref, cu_prev_ref):
    f32 = jnp.float32
    cw = CONV_WIDTH

    @pl.when(pl.program_id(1) == 0)
    def _():
        cu_prev_ref[...] = jnp.zeros(cu_prev_ref.shape, f32)

    def sigmoid(v):
        return 0.5 * jnp.tanh(0.5 * v) + 0.5

    def chain(ci, prev):
        rows = slice(ci * TM, (ci + 1) * TM)
        x = x_ref[rows, :]
        ms = jnp.mean(x * x, axis=-1, keepdims=True)
        h = (x * lax.rsqrt(ms + NORM_EPS)) * prew_ref[...]
        hb = h.astype(jnp.bfloat16)

        def cols(c0, width):
            return jnp.dot(hb, wrest_ref[:, c0:c0 + width], preferred_element_type=f32)

        c0 = ATT_WIDTH + 4 * cw
        for c in range(0, 2 * D_MODEL, 512):
            bias = mb_ref[c // D_MODEL:c // D_MODEL + 1, c % D_MODEL:c % D_MODEL + 512]
            gate_ref[rows, c:c + 512] = sigmoid(cols(c0 + c, 512) + bias).astype(jnp.bfloat16)

        z_att = cols(0, ATT_WIDTH)
        sz_ref[rows, :] = (z_att * sigmoid(z_att)).astype(jnp.bfloat16)

        c0 = ATT_WIDTH
        gb = cols(c0, cw)
        cu = cols(c0 + cw, cw) * cols(c0 + 2 * cw, cw)
        z_conv = cols(c0 + 3 * cw, cw)

        h1 = prev[7:8]
        h2 = prev[6:7]
        row = lax.broadcasted_iota(jnp.int32, (8, cw), 0)

        def shifted(by, fix_top):
            r = pltpu.roll(cu, by, 0)
            return jnp.concatenate([fix_top(r[0:8]), r[8:]], axis=0)

        r1 = shifted(1, lambda top: jnp.where(row == 0, h1, top))
        r2 = shifted(2, lambda top: jnp.where(row == 0, h2, jnp.where(row == 1, h1, top)))
        conv = cw_ref[0:1, :] * r2 + cw_ref[1:2, :] * r1 + cw_ref[2:3, :] * cu
        cvs_ref[rows, :] = ((gb * conv) * (z_conv * sigmoid(z_conv))).astype(jnp.bfloat16)

        qkT = lax.dot_general(wT_ref[0:2 * QK_WIDTH, :], hb, _NT, preferred_element_type=f32)

        pos = pos_ref[0, :, rows].astype(jnp.float32)
        ang = freq_ref[...] * pos
        cos = jnp.cos(ang)
        sin = jnp.sin(ang)

        def rot(blk):
            t1 = blk[0:ROPE_HALF]
            t2 = blk[ROPE_HALF:ROPE_DIM]
            return jnp.concatenate(
                [t1 * cos - t2 * sin, t2 * cos + t1 * sin, blk[ROPE_DIM:]], axis=0)

        scale = math.log2(math.e) / math.sqrt(HEAD_DIM)
        for hd in range(2 * HEADS):
            r0 = hd * HEAD_DIM
            qT_ref[0, r0:r0 + HEAD_DIM, rows] = (
                rot(qkT[r0:r0 + HEAD_DIM]) * scale).astype(jnp.bfloat16)
        kT = jnp.concatenate(
            [rot(qkT[QK_WIDTH + hd * HEAD_DIM:QK_WIDTH + (hd + 1) * HEAD_DIM])
             for hd in range(2 * HEADS)], axis=0)
        k = kT.T.astype(jnp.bfloat16)
        for p in range(HEADS):
            k_ref[0, p, rows, :] = k[:, p * V_DIM:(p + 1) * V_DIM]

        vT = lax.dot_general(wT_ref[2 * QK_WIDTH:QKV_WIDTH, :], hb, _NT,
                             preferred_element_type=f32)
        one_row = lax.broadcasted_iota(jnp.int32, (BF16_ROWS, TM), 0) == 0
        ones_rows = jnp.where(one_row, 1.0, 0.0).astype(jnp.bfloat16)
        for p in range(HEADS):
            vT_ref[0, ci, p * VX_DIM:p * VX_DIM + V_DIM, :] = (
                vT[p * V_DIM:(p + 1) * V_DIM].astype(jnp.bfloat16))
            vT_ref[0, ci, p * VX_DIM + V_DIM:(p + 1) * VX_DIM, :] = ones_rows
        return cu[TM - 8:TM]

    prev = cu_prev_ref[...]
    for ci in range(TMP // TM):
        prev = chain(ci, prev)
    cu_prev_ref[...] = prev


def _attn_kernel(klist_ref, nunm_ref, ntot_ref,
                 qT_ref, k_ref, vT_ref, posr_ref, posk_ref,
                 lq1_ref, lk1_ref, lq2_ref, lk2_ref, sw_ref, sz_ref,
                 o_ref, q2_ref, acc_ref, m_ref, sa_ref, sb_ref, mt0_ref,
                 *, nq, nk, lambda_init):
    qt = pl.program_id(0) * nq + pl.program_id(1)
    n = ntot_ref[qt]
    n_unm = nunm_ref[qt]

    qc = lax.shift_right_arithmetic(posr_ref[0], CHUNK_SHIFT)
    qc2 = jnp.concatenate([qc, qc], axis=1)

    def init_state():
        for p in range(HEADS):
            q2_ref[p] = jnp.zeros(q2_ref.shape[1:], q2_ref.dtype)
            q2_ref[p, 0:HEAD_DIM, 0:TQ] = qT_ref[0, p * V_DIM:p * V_DIM + HEAD_DIM, :]
            q2_ref[p, HEAD_DIM:V_DIM, TQ:2 * TQ] = (
                qT_ref[0, p * V_DIM + HEAD_DIM:(p + 1) * V_DIM, :])
            m_ref[p] = jnp.full(m_ref.shape[1:], NEG, jnp.float32)
            acc_ref[p] = jnp.zeros(acc_ref.shape[1:], jnp.float32)

    def key_tile(r):
        return klist_ref[qt * nk + r]

    def score(p, j, s_ref, masked):
        k0 = pl.multiple_of(j * TK, TK)
        s = jnp.dot(k_ref[0, p, pl.ds(k0, TK), :], q2_ref[p],
                    preferred_element_type=jnp.float32)
        if masked:
            kc = lax.shift_right_arithmetic(posk_ref[0, j], CHUNK_SHIFT)
            kc = jnp.broadcast_to(kc, (LANES, TK)).T
            kc = jnp.concatenate([kc] * (2 * TQ // LANES), axis=1)
            s = jnp.where(kc <= qc2, s, NEG)
        s_ref[...] = s
        return jnp.max(s, axis=0, keepdims=True)

    def update(p, j, s_ref, mt):
        m_old = m_ref[p]
        m_new = jnp.maximum(m_old, mt)
        alpha = jnp.exp2(m_old - m_new)
        pr = jnp.exp2(s_ref[...] - m_new).astype(jnp.bfloat16)
        vt = vT_ref[0, j, p * VX_DIM:(p + 1) * VX_DIM, :]
        acc_ref[p] = alpha * acc_ref[p] + jnp.dot(vt, pr, preferred_element_type=jnp.float32)
        m_ref[p] = m_new

    def block(j, j_next, masked, next_masked):
        mt1 = score(1, j, sb_ref, masked)
        update(0, j, sa_ref, mt0_ref[...])
        mt2 = score(2, j, sa_ref, masked)
        update(1, j, sb_ref, mt1)
        mt3 = score(3, j, sb_ref, masked)
        update(2, j, sa_ref, mt2)
        if next_masked is not None:
            mt0_ref[...] = score(0, j_next, sa_ref, next_masked)
        update(3, j, sb_ref, mt3)

    @pl.when(n_unm > 0)
    def _():
        init_state()
        mt0_ref[...] = score(0, key_tile(0), sa_ref, False)

    @pl.when(n_unm == 0)
    def _():
        init_state()
        mt0_ref[...] = score(0, key_tile(0), sa_ref, True)

    n_double = jnp.maximum(n_unm - 1, 0) // 2

    def free_blocks_2(r, carry):
        block(key_tile(2 * r), key_tile(2 * r + 1), False, False)
        block(key_tile(2 * r + 1), key_tile(2 * r + 2), False, False)
        return carry

    lax.fori_loop(0, n_double, free_blocks_2, 0)

    def free_block(r, carry):
        block(key_tile(r), key_tile(r + 1), False, False)
        return carry

    lax.fori_loop(2 * n_double, n_unm - 1, free_block, 0)

    @pl.when((n_unm > 0) & (n > n_unm))
    def _():
        block(key_tile(n_unm - 1), key_tile(n_unm), False, True)

    @pl.when((n_unm > 0) & (n == n_unm))
    def _():
        block(key_tile(n_unm - 1), None, False, None)

    def masked_block(r, carry):
        block(key_tile(r), key_tile(r + 1), True, True)
        return carry

    lax.fori_loop(n_unm, n - 1, masked_block, 0)

    @pl.when(n > n_unm)
    def _():
        block(key_tile(n - 1), None, True, None)

    lam = (jnp.exp(jnp.sum(lq1_ref[...] * lk1_ref[...], axis=-1, keepdims=True))
           - jnp.exp(jnp.sum(lq2_ref[...] * lk2_ref[...], axis=-1, keepdims=True))
           + lambda_init)
    for p in range(HEADS):
        o = acc_ref[p, 0:V_DIM, :] / acc_ref[p, V_DIM:V_DIM + 1, :]
        d = o[:, :TQ] - lam * o[:, TQ:]
        ms = jnp.mean(d * d, axis=0, keepdims=True)
        dn = ((d * lax.rsqrt(ms + SUBLN_EPS)) * sw_ref[...]) * (1.0 - lambda_init)
        lanes = slice(p * V_DIM, (p + 1) * V_DIM)
        o_ref[:, lanes] = (dn.T * sz_ref[:, lanes].astype(jnp.float32)).astype(o_ref.dtype)


def _out_kernel(x_ref, att_ref, cvs_ref, gate_ref, wao_ref, wco_ref, wo_ref, postw_ref, o_ref):
    f32 = jnp.float32
    for r0 in range(0, TMO, OUT_ROWS):
        rows = slice(r0, r0 + OUT_ROWS)
        y_att = jnp.dot(att_ref[rows, :], wao_ref[...], preferred_element_type=f32)
        y_conv = jnp.dot(cvs_ref[rows, :], wco_ref[...], preferred_element_type=f32)
        m = (gate_ref[rows, 0:D_MODEL].astype(f32) * y_att
             + gate_ref[rows, D_MODEL:2 * D_MODEL].astype(f32) * y_conv)
        o = jnp.dot(m.astype(jnp.bfloat16), wo_ref[...], preferred_element_type=f32)
        ms = jnp.mean(o * o, axis=-1, keepdims=True)
        o_ref[rows, :] = x_ref[rows, :] + (o * lax.rsqrt(ms + NORM_EPS)) * postw_ref[...]


def _key_tile_lists(positions, batch, nq, nk):
    chunk = lax.shift_right_arithmetic(positions, CHUNK_SHIFT)
    kch = chunk.reshape(batch, 1, nk, TK)
    qch = chunk.reshape(batch, nq, 1, TQ)
    needed = kch.min(-1) <= qch.max(-1)
    nomask = needed & (kch.max(-1) <= qch.min(-1))
    rank = jnp.where(nomask, 0, jnp.where(needed, 1, 2))
    order = jnp.argsort(rank, axis=-1, stable=True).astype(jnp.int32)

    def count(flag):
        return flag.sum(-1).astype(jnp.int32).reshape(-1)

    return order.reshape(-1), count(nomask), count(needed)


def _layer(x2, positions, pre_w, w_in, merge_bias, lq1, lk1, lq2, lk2, subln_w,
           w_att_out, conv_w, w_conv_out, w_out, post_w, lambda_init, batch, seq):
    bf16 = jnp.bfloat16
    n = batch * seq
    nt = seq // TMP
    nq = seq // TQ
    nk = seq // TK
    assert seq % TMP == 0 and seq % TQ == 0 and TM == TK and n % TMO == 0

    freqs = (ROPE_THETA ** (-jnp.arange(0, ROPE_DIM, 2, dtype=jnp.float32) / ROPE_DIM)
             ).reshape(ROPE_HALF, 1)
    n_t = QKV_WIDTH // WPREP_COLS
    wT, wrest = pl.pallas_call(
        functools.partial(_wprep_kernel, n_transposed=n_t),
        grid=((QKV_WIDTH + REST_WIDTH) // WPREP_COLS,),
        in_specs=[pl.BlockSpec((D_MODEL, WPREP_COLS), lambda c: (0, c))],
        out_specs=[
            pl.BlockSpec((WPREP_COLS, D_MODEL), lambda c: (jnp.minimum(c, n_t - 1), 0)),
            pl.BlockSpec((D_MODEL, WPREP_COLS), lambda c: (0, jnp.maximum(c - n_t, 0))),
        ],
        out_shape=[
            jax.ShapeDtypeStruct((QKV_WIDTH, D_MODEL), bf16),
            jax.ShapeDtypeStruct((D_MODEL, REST_WIDTH), bf16),
        ],
        compiler_params=pltpu.CompilerParams(dimension_semantics=("arbitrary",)),
        name="wprep",
    )(w_in)
    pos_row = positions.reshape(batch, 1, seq)
    pos_tiles = positions.reshape(batch, nk, 1, TK)

    cparams = functools.partial(pltpu.CompilerParams, vmem_limit_bytes=VMEM_LIMIT)

    def rows(width):
        return pl.BlockSpec((TMP, width), lambda b, t: (b * nt + t, 0))

    qT, k, vT, sz, cvs, gates = pl.pallas_call(
        _proj_kernel,
        grid=(batch, nt),
        in_specs=[
            rows(D_MODEL),
            pl.BlockSpec((1, 1, TMP), lambda b, t: (b, 0, t)),
            _resident((ROPE_HALF, 1), lambda b, t: (0, 0)),
            _resident((1, D_MODEL), lambda b, t: (0, 0)),
            _resident((QKV_WIDTH, D_MODEL), lambda b, t: (0, 0)),
            _resident((D_MODEL, REST_WIDTH), lambda b, t: (0, 0)),
            _resident((2, D_MODEL), lambda b, t: (0, 0)),
            _resident((CONV_K, CONV_WIDTH), lambda b, t: (0, 0)),
        ],
        out_specs=[
            pl.BlockSpec((1, QK_WIDTH, TMP), lambda b, t: (b, 0, t)),
            pl.BlockSpec((1, HEADS, TMP, V_DIM), lambda b, t: (b, 0, t, 0)),
            pl.BlockSpec((1, TMP // TK, HEADS * VX_DIM, TK), lambda b, t: (b, t, 0, 0)),
            rows(ATT_WIDTH),
            rows(CONV_WIDTH),
            rows(2 * D_MODEL),
        ],
        out_shape=[
            jax.ShapeDtypeStruct((batch, QK_WIDTH, seq), bf16),
            jax.ShapeDtypeStruct((batch, HEADS, seq, V_DIM), bf16),
            jax.ShapeDtypeStruct((batch, nk, HEADS * VX_DIM, TK), bf16),
            jax.ShapeDtypeStruct((n, ATT_WIDTH), bf16),
            jax.ShapeDtypeStruct((n, CONV_WIDTH), bf16),
            jax.ShapeDtypeStruct((n, 2 * D_MODEL), bf16),
        ],
        scratch_shapes=[pltpu.VMEM((8, CONV_WIDTH), jnp.float32)],
        compiler_params=cparams(dimension_semantics=("arbitrary", "arbitrary")),
        name="proj",
    )(x2, pos_row, freqs, pre_w.reshape(1, D_MODEL), wT, wrest,
      merge_bias, conv_w.reshape(CONV_K, CONV_WIDTH))

    klist, n_unm, n_tot = _key_tile_lists(positions, batch, nq, nk)

    att = pl.pallas_call(
        functools.partial(_attn_kernel, nq=nq, nk=nk, lambda_init=lambda_init),
        grid_spec=pltpu.PrefetchScalarGridSpec(
            num_scalar_prefetch=3,
            grid=(batch, nq),
            in_specs=[
                pl.BlockSpec((1, QK_WIDTH, TQ), lambda b, i, *_: (b, 0, i)),
                pl.BlockSpec((1, HEADS, seq, V_DIM), lambda b, i, *_: (b, 0, 0, 0)),
                pl.BlockSpec((1, nk, HEADS * VX_DIM, TK), lambda b, i, *_: (b, 0, 0, 0)),
                pl.BlockSpec((1, 1, TQ), lambda b, i, *_: (b, 0, i)),
                pl.BlockSpec((1, nk, 1, TK), lambda b, i, *_: (b, 0, 0, 0)),
                _resident((1, HEAD_DIM), lambda b, i, *_: (0, 0)),
                _resident((1, HEAD_DIM), lambda b, i, *_: (0, 0)),
                _resident((1, HEAD_DIM), lambda b, i, *_: (0, 0)),
                _resident((1, HEAD_DIM), lambda b, i, *_: (0, 0)),
                _resident((V_DIM, 1), lambda b, i, *_: (0, 0)),
                pl.BlockSpec((TQ, ATT_WIDTH), lambda b, i, *_: (b * nq + i, 0)),
            ],
            out_specs=pl.BlockSpec((TQ, ATT_WIDTH), lambda b, i, *_: (b * nq + i, 0)),
            scratch_shapes=[
                pltpu.VMEM((HEADS, V_DIM, 2 * TQ), bf16),
                pltpu.VMEM((HEADS, VX_DIM, 2 * TQ), jnp.float32),
                pltpu.VMEM((HEADS, 1, 2 * TQ), jnp.float32),
                pltpu.VMEM((TK, 2 * TQ), jnp.float32),
                pltpu.VMEM((TK, 2 * TQ), jnp.float32),
                pltpu.VMEM((1, 2 * TQ), jnp.float32),
            ],
        ),
        out_shape=jax.ShapeDtypeStruct((n, ATT_WIDTH), bf16),
        compiler_params=cparams(dimension_semantics=("arbitrary", "arbitrary")),
        name="attn",
    )(klist, n_unm, n_tot, qT, k, vT, pos_row, pos_tiles,
      lq1.reshape(1, HEAD_DIM), lk1.reshape(1, HEAD_DIM),
      lq2.reshape(1, HEAD_DIM), lk2.reshape(1, HEAD_DIM), subln_w.reshape(V_DIM, 1), sz)

    out = pl.pallas_call(
        _out_kernel,
        grid=(n // TMO,),
        in_specs=[
            pl.BlockSpec((TMO, D_MODEL), lambda t: (t, 0)),
            pl.BlockSpec((TMO, ATT_WIDTH), lambda t: (t, 0)),
            pl.BlockSpec((TMO, CONV_WIDTH), lambda t: (t, 0)),
            pl.BlockSpec((TMO, 2 * D_MODEL), lambda t: (t, 0)),
            _resident((ATT_WIDTH, D_MODEL), lambda t: (0, 0)),
            _resident((CONV_WIDTH, D_MODEL), lambda t: (0, 0)),
            _resident((D_MODEL, D_MODEL), lambda t: (0, 0)),
            _resident((1, D_MODEL), lambda t: (0, 0)),
        ],
        out_specs=pl.BlockSpec((TMO, D_MODEL), lambda t: (t, 0)),
        out_shape=jax.ShapeDtypeStruct((n, D_MODEL), jnp.float32),
        compiler_params=cparams(dimension_semantics=("arbitrary",)),
        name="outp",
    )(x2, att, cvs, gates, w_att_out.astype(bf16), w_conv_out.astype(bf16), w_out.astype(bf16),
      post_w.reshape(1, D_MODEL))
    return out


def kernel(x, positions, pre_norm_w, w_in, merge_bias, lambda_q1, lambda_k1, lambda_q2,
           lambda_k2, subln_w, w_att_out, conv_w, w_conv_out, w_out, post_norm_w):
    batch, seq, _ = x.shape
    depth = w_in.shape[0]
    x2 = x.reshape(batch * seq, D_MODEL)
    for layer in range(depth):
        lambda_init = 0.8 - 0.6 * math.exp(-0.3 * layer)
        x2 = _layer(x2, positions, pre_norm_w[layer], w_in[layer], merge_bias[layer],
                    lambda_q1[layer], lambda_k1[layer], lambda_q2[layer], lambda_k2[layer],
                    subln_w[layer], w_att_out[layer], conv_w[layer], w_conv_out[layer],
                    w_out[layer], post_norm_w[layer], lambda_init, batch, seq)
    return x2.reshape(batch, seq, D_MODEL)
```

```python
import functools
import math

import jax
import jax.numpy as jnp
from jax import lax
from jax.experimental import pallas as pl
from jax.experimental.pallas import tpu as pltpu

D_MODEL = 1024
CHUNK_SHIFT = 6
HEADS = 4
HEAD_DIM = 64
V_DIM = 2 * HEAD_DIM
LANES = 128
BF16_ROWS = 16
VX_DIM = V_DIM + BF16_ROWS
QK_WIDTH = 2 * HEADS * HEAD_DIM
ATT_WIDTH = HEADS * V_DIM
CONV_WIDTH = D_MODEL // 2
CONV_K = 3
ROPE_THETA = 500000.0
ROPE_DIM = HEAD_DIM // 4
ROPE_HALF = ROPE_DIM // 2
NORM_EPS = 1e-6
SUBLN_EPS = 1e-5
REST_WIDTH = ATT_WIDTH + 4 * CONV_WIDTH + 2 * D_MODEL
QKV_WIDTH = 2 * QK_WIDTH + ATT_WIDTH
WPREP_COLS = 512

TM = 512
TMP = 1024
TQ = 512
TK = 512
FREE_TILE_GROUPS = (2, 1)
TMO = 1024
OUT_ROWS = 256

VMEM_LIMIT = 56 * 1024 * 1024
NEG = float(jnp.finfo(jnp.float32).min) / 2

_NT = (((1,), (1,)), ((), ()))


def _resident(block_shape, index_map):
    return pl.BlockSpec(block_shape, index_map, pipeline_mode=pl.Buffered(1))


def _wprep_kernel(w_ref, wT_ref, wrest_ref, *, n_transposed):
    c = pl.program_id(0)

    @pl.when(c < n_transposed)
    def _():
        wT_ref[...] = w_ref[...].T.astype(jnp.bfloat16)

    @pl.when(c >= n_transposed)
    def _():
        wrest_ref[...] = w_ref[...].astype(jnp.bfloat16)


def _proj_kernel(x_ref, pos_ref, freq_ref, prew_ref, wT_ref, wrest_ref, mb_ref, cw_ref,
                 qT_ref, k_ref, vT_ref, sz_ref, cvs_ref, gate_ref, cu_prev_ref):
    f32 = jnp.float32
    cw = CONV_WIDTH

    @pl.when(pl.program_id(1) == 0)
    def _():
        cu_prev_ref[...] = jnp.zeros(cu_prev_ref.shape, f32)

    def sigmoid(v):
        return 0.5 * jnp.tanh(0.5 * v) + 0.5

    def chain(ci, prev):
        rows = slice(ci * TM, (ci + 1) * TM)
        x = x_ref[rows, :]
        ms = jnp.mean(x * x, axis=-1, keepdims=True)
        h = (x * lax.rsqrt(ms + NORM_EPS)) * prew_ref[...]
        hb = h.astype(jnp.bfloat16)

        def cols(c0, width):
            return jnp.dot(hb, wrest_ref[:, c0:c0 + width], preferred_element_type=f32)

        c0 = ATT_WIDTH + 4 * cw
        for c in range(0, 2 * D_MODEL, 512):
            bias = mb_ref[c // D_MODEL:c // D_MODEL + 1, c % D_MODEL:c % D_MODEL + 512]
            gate_ref[rows, c:c + 512] = sigmoid(cols(c0 + c, 512) + bias).astype(jnp.bfloat16)

        z_att = cols(0, ATT_WIDTH)
        sz_ref[rows, :] = (z_att * sigmoid(z_att)).astype(jnp.bfloat16)

        c0 = ATT_WIDTH
        gb = cols(c0, cw)
        cu = cols(c0 + cw, cw) * cols(c0 + 2 * cw, cw)
        z_conv = cols(c0 + 3 * cw, cw)

        h1 = prev[7:8]
        h2 = prev[6:7]
        row = lax.broadcasted_iota(jnp.int32, (8, cw), 0)

        def shifted(by, fix_top):
            r = pltpu.roll(cu, by, 0)
            return jnp.concatenate([fix_top(r[0:8]), r[8:]], axis=0)

        r1 = shifted(1, lambda top: jnp.where(row == 0, h1, top))
        r2 = shifted(2, lambda top: jnp.where(row == 0, h2, jnp.where(row == 1, h1, top)))
        conv = cw_ref[0:1, :] * r2 + cw_ref[1:2, :] * r1 + cw_ref[2:3, :] * cu
        cvs_ref[rows, :] = ((gb * conv) * (z_conv * sigmoid(z_conv))).astype(jnp.bfloat16)

        qkT = lax.dot_general(wT_ref[0:2 * QK_WIDTH, :], hb, _NT, preferred_element_type=f32)

        pos = pos_ref[0, :, rows].astype(jnp.float32)
        ang = freq_ref[...] * pos
        cos = jnp.cos(ang)
        sin = jnp.sin(ang)

        def rot(blk):
            t1 = blk[0:ROPE_HALF]
            t2 = blk[ROPE_HALF:ROPE_DIM]
            return jnp.concatenate(
                [t1 * cos - t2 * sin, t2 * cos + t1 * sin, blk[ROPE_DIM:]], axis=0)

        scale = math.log2(math.e) / math.sqrt(HEAD_DIM)
        for hd in range(2 * HEADS):
            r0 = hd * HEAD_DIM
            qT_ref[0, r0:r0 + HEAD_DIM, rows] = (
                rot(qkT[r0:r0 + HEAD_DIM]) * scale).astype(jnp.bfloat16)
        kT = jnp.concatenate(
            [rot(qkT[QK_WIDTH + hd * HEAD_DIM:QK_WIDTH + (hd + 1) * HEAD_DIM])
             for hd in range(2 * HEADS)], axis=0)
        k = kT.T.astype(jnp.bfloat16)
        for p in range(HEADS):
            k_ref[0, p, rows, :] = k[:, p * V_DIM:(p + 1) * V_DIM]

        vT = lax.dot_general(wT_ref[2 * QK_WIDTH:QKV_WIDTH, :], hb, _NT,
                             preferred_element_type=f32)
        one_row = lax.broadcasted_iota(jnp.int32, (BF16_ROWS, TM), 0) == 0
        ones_rows = jnp.where(one_row, 1.0, 0.0).astype(jnp.bfloat16)
        for p in range(HEADS):
            vT_ref[0, ci, p * VX_DIM:p * VX_DIM + V_DIM, :] = (
                vT[p * V_DIM:(p + 1) * V_DIM].astype(jnp.bfloat16))
            vT_ref[0, ci, p * VX_DIM + V_DIM:(p + 1) * VX_DIM, :] = ones_rows
        return cu[TM - 8:TM]

    prev = cu_prev_ref[...]
    for ci in range(TMP // TM):
        prev = chain(ci, prev)
    cu_prev_ref[...] = prev


def _attn_kernel(klist_ref, nunm_ref, ntot_ref,
                 qT_ref, k_ref, vT_ref, posr_ref, posk_ref,
                 lq1_ref, lk1_ref, lq2_ref, lk2_ref, sw_ref, sz_ref,
                 o_ref, q2_ref, acc_ref, m_ref, sa_ref, sb_ref, mt0_ref,
                 *, nq, nk, lambda_init):
    qt = pl.program_id(0) * nq + pl.program_id(1)
    n = ntot_ref[qt]
    n_unm = nunm_ref[qt]

    qc = lax.shift_right_arithmetic(posr_ref[0], CHUNK_SHIFT)
    qc2 = jnp.concatenate([qc, qc], axis=1)

    def init_state():
        for p in range(HEADS):
            q2_ref[p] = jnp.zeros(q2_ref.shape[1:], q2_ref.dtype)
            q2_ref[p, 0:HEAD_DIM, 0:TQ] = qT_ref[0, p * V_DIM:p * V_DIM + HEAD_DIM, :]
            q2_ref[p, HEAD_DIM:V_DIM, TQ:2 * TQ] = (
                qT_ref[0, p * V_DIM + HEAD_DIM:(p + 1) * V_DIM, :])
            m_ref[p] = jnp.full(m_ref.shape[1:], NEG, jnp.float32)
            acc_ref[p] = jnp.zeros(acc_ref.shape[1:], jnp.float32)

    def key_tile(r):
        return klist_ref[qt * nk + r]

    def score(p, j, s_ref, masked):
        k0 = pl.multiple_of(j * TK, TK)
        s = jnp.dot(k_ref[0, p, pl.ds(k0, TK), :], q2_ref[p],
                    preferred_element_type=jnp.float32)
        if masked:
            kc = lax.shift_right_arithmetic(posk_ref[0, j], CHUNK_SHIFT)
            kc = jnp.broadcast_to(kc, (LANES, TK)).T
            kc = jnp.concatenate([kc] * (2 * TQ // LANES), axis=1)
            s = jnp.where(kc <= qc2, s, NEG)
        s_ref[...] = s
        return jnp.max(s, axis=0, keepdims=True)

    def update(p, j, s_ref, mt):
        m_old = m_ref[p]
        m_new = jnp.maximum(m_old, mt)
        alpha = jnp.exp2(m_old - m_new)
        pr = jnp.exp2(s_ref[...] - m_new).astype(jnp.bfloat16)
        vt = vT_ref[0, j, p * VX_DIM:(p + 1) * VX_DIM, :]
        acc_ref[p] = alpha * acc_ref[p] + jnp.dot(vt, pr, preferred_element_type=jnp.float32)
        m_ref[p] = m_new

    def block(j, j_next, masked, next_masked):
        mt1 = score(1, j, sb_ref, masked)
        update(0, j, sa_ref, mt0_ref[...])
        mt2 = score(2, j, sa_ref, masked)
        update(1, j, sb_ref, mt1)
        mt3 = score(3, j, sb_ref, masked)
        update(2, j, sa_ref, mt2)
        if next_masked is not None:
            mt0_ref[...] = score(0, j_next, sa_ref, next_masked)
        update(3, j, sb_ref, mt3)

    @pl.when(n_unm > 0)
    def _():
        init_state()
        mt0_ref[...] = score(0, key_tile(0), sa_ref, False)

    @pl.when(n_unm == 0)
    def _():
        init_state()
        mt0_ref[...] = score(0, key_tile(0), sa_ref, True)

    done = 0
    for group in FREE_TILE_GROUPS:
        n_groups = jnp.maximum(n_unm - 1 - done, 0) // group

        def free_blocks(g, carry, group=group, done=done):
            r0 = done + g * group
            for r in range(group):
                block(key_tile(r0 + r), key_tile(r0 + r + 1), False, False)
            return carry

        lax.fori_loop(0, n_groups, free_blocks, 0)
        done = done + n_groups * group

    @pl.when((n_unm > 0) & (n > n_unm))
    def _():
        block(key_tile(n_unm - 1), key_tile(n_unm), False, True)

    @pl.when((n_unm > 0) & (n == n_unm))
    def _():
        block(key_tile(n_unm - 1), None, False, None)

    def masked_block(r, carry):
        block(key_tile(r), key_tile(r + 1), True, True)
        return carry

    lax.fori_loop(n_unm, n - 1, masked_block, 0)

    @pl.when(n > n_unm)
    def _():
        block(key_tile(n - 1), None, True, None)

    lam = (jnp.exp(jnp.sum(lq1_ref[...] * lk1_ref[...], axis=-1, keepdims=True))
           - jnp.exp(jnp.sum(lq2_ref[...] * lk2_ref[...], axis=-1, keepdims=True))
           + lambda_init)
    for p in range(HEADS):
        o = acc_ref[p, 0:V_DIM, :] / acc_ref[p, V_DIM:V_DIM + 1, :]
        d = o[:, :TQ] - lam * o[:, TQ:]
        ms = jnp.mean(d * d, axis=0, keepdims=True)
        dn = ((d * lax.rsqrt(ms + SUBLN_EPS)) * sw_ref[...]) * (1.0 - lambda_init)
        lanes = slice(p * V_DIM, (p + 1) * V_DIM)
        o_ref[:, lanes] = (dn.T * sz_ref[:, lanes].astype(jnp.float32)).astype(o_ref.dtype)


def _out_kernel(x_ref, att_ref, cvs_ref, gate_ref, wao_ref, wco_ref, wo_ref, postw_ref, o_ref):
    f32 = jnp.float32
    for r0 in range(0, TMO, OUT_ROWS):
        rows = slice(r0, r0 + OUT_ROWS)
        y_att = jnp.dot(att_ref[rows, :], wao_ref[...], preferred_element_type=f32)
        y_conv = jnp.dot(cvs_ref[rows, :], wco_ref[...], preferred_element_type=f32)
        m = (gate_ref[rows, 0:D_MODEL].astype(f32) * y_att
             + gate_ref[rows, D_MODEL:2 * D_MODEL].astype(f32) * y_conv)
        o = jnp.dot(m.astype(jnp.bfloat16), wo_ref[...], preferred_element_type=f32)
        ms = jnp.mean(o * o, axis=-1, keepdims=True)
        o_ref[rows, :] = x_ref[rows, :] + (o * lax.rsqrt(ms + NORM_EPS)) * postw_ref[...]


def _key_tile_lists(positions, batch, nq, nk):
    chunk = lax.shift_right_arithmetic(positions, CHUNK_SHIFT)
    kch = chunk.reshape(batch, 1, nk, TK)
    qch = chunk.reshape(batch, nq, 1, TQ)
    needed = kch.min(-1) <= qch.max(-1)
    nomask = needed & (kch.max(-1) <= qch.min(-1))
    rank = jnp.where(nomask, 0, jnp.where(needed, 1, 2))
    order = jnp.argsort(rank, axis=-1, stable=True).astype(jnp.int32)

    def count(flag):
        return flag.sum(-1).astype(jnp.int32).reshape(-1)

    return order.reshape(-1), count(nomask), count(needed)


def _layer(x2, positions, pre_w, w_in, merge_bias, lq1, lk1, lq2, lk2, subln_w,
           w_att_out, conv_w, w_conv_out, w_out, post_w, lambda_init, batch, seq):
    bf16 = jnp.bfloat16
    n = batch * seq
    nt = seq // TMP
    nq = seq // TQ
    nk = seq // TK
    assert seq % TMP == 0 and seq % TQ == 0 and TM == TK and n % TMO == 0

    freqs = (ROPE_THETA ** (-jnp.arange(0, ROPE_DIM, 2, dtype=jnp.float32) / ROPE_DIM)
             ).reshape(ROPE_HALF, 1)
    n_t = QKV_WIDTH // WPREP_COLS
    wT, wrest = pl.pallas_call(
        functools.partial(_wprep_kernel, n_transposed=n_t),
        grid=((QKV_WIDTH + REST_WIDTH) // WPREP_COLS,),
        in_specs=[pl.BlockSpec((D_MODEL, WPREP_COLS), lambda c: (0, c))],
        out_specs=[
            pl.BlockSpec((WPREP_COLS, D_MODEL), lambda c: (jnp.minimum(c, n_t - 1), 0)),
            pl.BlockSpec((D_MODEL, WPREP_COLS), lambda c: (0, jnp.maximum(c - n_t, 0))),
        ],
        out_shape=[
            jax.ShapeDtypeStruct((QKV_WIDTH, D_MODEL), bf16),
            jax.ShapeDtypeStruct((D_MODEL, REST_WIDTH), bf16),
        ],
        compiler_params=pltpu.CompilerParams(dimension_semantics=("arbitrary",)),
        name="wprep",
    )(w_in)
    pos_row = positions.reshape(batch, 1, seq)
    pos_tiles = positions.reshape(batch, nk, 1, TK)

    cparams = functools.partial(pltpu.CompilerParams, vmem_limit_bytes=VMEM_LIMIT)

    def rows(width):
        return pl.BlockSpec((TMP, width), lambda b, t: (b * nt + t, 0))

    qT, k, vT, sz, cvs, gates = pl.pallas_call(
        _proj_kernel,
        grid=(batch, nt),
        in_specs=[
            rows(D_MODEL),
            pl.BlockSpec((1, 1, TMP), lambda b, t: (b, 0, t)),
            _resident((ROPE_HALF, 1), lambda b, t: (0, 0)),
            _resident((1, D_MODEL), lambda b, t: (0, 0)),
            _resident((QKV_WIDTH, D_MODEL), lambda b, t: (0, 0)),
            _resident((D_MODEL, REST_WIDTH), lambda b, t: (0, 0)),
            _resident((2, D_MODEL), lambda b, t: (0, 0)),
            _resident((CONV_K, CONV_WIDTH), lambda b, t: (0, 0)),
        ],
        out_specs=[
            pl.BlockSpec((1, QK_WIDTH, TMP), lambda b, t: (b, 0, t)),
            pl.BlockSpec((1, HEADS, TMP, V_DIM), lambda b, t: (b, 0, t, 0)),
            pl.BlockSpec((1, TMP // TK, HEADS * VX_DIM, TK), lambda b, t: (b, t, 0, 0)),
            rows(ATT_WIDTH),
            rows(CONV_WIDTH),
            rows(2 * D_MODEL),
        ],
        out_shape=[
            jax.ShapeDtypeStruct((batch, QK_WIDTH, seq), bf16),
            jax.ShapeDtypeStruct((batch, HEADS, seq, V_DIM), bf16),
            jax.ShapeDtypeStruct((batch, nk, HEADS * VX_DIM, TK), bf16),
            jax.ShapeDtypeStruct((n, ATT_WIDTH), bf16),
            jax.ShapeDtypeStruct((n, CONV_WIDTH), bf16),
            jax.ShapeDtypeStruct((n, 2 * D_MODEL), bf16),
        ],
        scratch_shapes=[pltpu.VMEM((8, CONV_WIDTH), jnp.float32)],
        compiler_params=cparams(dimension_semantics=("arbitrary", "arbitrary")),
        name="proj",
    )(x2, pos_row, freqs, pre_w.reshape(1, D_MODEL), wT, wrest,
      merge_bias, conv_w.reshape(CONV_K, CONV_WIDTH))

    klist, n_unm, n_tot = _key_tile_lists(positions, batch, nq, nk)

    att = pl.pallas_call(
        functools.partial(_attn_kernel, nq=nq, nk=nk, lambda_init=lambda_init),
        grid_spec=pltpu.PrefetchScalarGridSpec(
            num_scalar_prefetch=3,
            grid=(batch, nq),
            in_specs=[
                pl.BlockSpec((1, QK_WIDTH, TQ), lambda b, i, *_: (b, 0, i)),
                pl.BlockSpec((1, HEADS, seq, V_DIM), lambda b, i, *_: (b, 0, 0, 0)),
                pl.BlockSpec((1, nk, HEADS * VX_DIM, TK), lambda b, i, *_: (b, 0, 0, 0)),
                pl.BlockSpec((1, 1, TQ), lambda b, i, *_: (b, 0, i)),
                pl.BlockSpec((1, nk, 1, TK), lambda b, i, *_: (b, 0, 0, 0)),
                _resident((1, HEAD_DIM), lambda b, i, *_: (0, 0)),
                _resident((1, HEAD_DIM), lambda b, i, *_: (0, 0)),
                _resident((1, HEAD_DIM), lambda b, i, *_: (0, 0)),
                _resident((1, HEAD_DIM), lambda b, i, *_: (0, 0)),
                _resident((V_DIM, 1), lambda b, i, *_: (0, 0)),
                pl.BlockSpec((TQ, ATT_WIDTH), lambda b, i, *_: (b * nq + i, 0)),
            ],
            out_specs=pl.BlockSpec((TQ, ATT_WIDTH), lambda b, i, *_: (b * nq + i, 0)),
            scratch_shapes=[
                pltpu.VMEM((HEADS, V_DIM, 2 * TQ), bf16),
                pltpu.VMEM((HEADS, VX_DIM, 2 * TQ), jnp.float32),
                pltpu.VMEM((HEADS, 1, 2 * TQ), jnp.float32),
                pltpu.VMEM((TK, 2 * TQ), jnp.float32),
                pltpu.VMEM((TK, 2 * TQ), jnp.float32),
                pltpu.VMEM((1, 2 * TQ), jnp.float32),
            ],
        ),
        out_shape=jax.ShapeDtypeStruct((n, ATT_WIDTH), bf16),
        compiler_params=cparams(dimension_semantics=("arbitrary", "arbitrary")),
        name="attn",
    )(klist, n_unm, n_tot, qT, k, vT, pos_row, pos_tiles,
      lq1.reshape(1, HEAD_DIM), lk1.reshape(1, HEAD_DIM),
      lq2.reshape(1, HEAD_DIM), lk2.reshape(1, HEAD_DIM), subln_w.reshape(V_DIM, 1), sz)

    out = pl.pallas_call(
        _out_kernel,
        grid=(n // TMO,),
        in_specs=[
            pl.BlockSpec((TMO, D_MODEL), lambda t: (t, 0)),
            pl.BlockSpec((TMO, ATT_WIDTH), lambda t: (t, 0)),
            pl.BlockSpec((TMO, CONV_WIDTH), lambda t: (t, 0)),
            pl.BlockSpec((TMO, 2 * D_MODEL), lambda t: (t, 0)),
            _resident((ATT_WIDTH, D_MODEL), lambda t: (0, 0)),
            _resident((CONV_WIDTH, D_MODEL), lambda t: (0, 0)),
            _resident((D_MODEL, D_MODEL), lambda t: (0, 0)),
            _resident((1, D_MODEL), lambda t: (0, 0)),
        ],
        out_specs=pl.BlockSpec((TMO, D_MODEL), lambda t: (t, 0)),
        out_shape=jax.ShapeDtypeStruct((n, D_MODEL), jnp.float32),
        compiler_params=cparams(dimension_semantics=("arbitrary",)),
        name="outp",
    )(x2, att, cvs, gates, w_att_out.astype(bf16), w_conv_out.astype(bf16), w_out.astype(bf16),
      post_w.reshape(1, D_MODEL))
    return out


def kernel(x, positions, pre_norm_w, w_in, merge_bias, lambda_q1, lambda_k1, lambda_q2,
           lambda_k2, subln_w, w_att_out, conv_w, w_conv_out, w_out, post_norm_w):
    batch, seq, _ = x.shape
    depth = w_in.shape[0]
    x2 = x.reshape(batch * seq, D_MODEL)
    for layer in range(depth):
        lambda_init = 0.8 - 0.6 * math.exp(-0.3 * layer)
        x2 = _layer(x2, positions, pre_norm_w[layer], w_in[layer], merge_bias[layer],
                    lambda_q1[layer], lambda_k1[layer], lambda_q2[layer], lambda_k2[layer],
                    subln_w[layer], w_att_out[layer], conv_w[layer], w_conv_out[layer],
                    w_out[layer], post_norm_w[layer], lambda_init, batch, seq)
    return x2.reshape(batch, seq, D_MODEL)
```

```python
import functools
import math

import jax
import jax.numpy as jnp
from jax import lax
from jax.experimental import pallas as pl
from jax.experimental.pallas import tpu as pltpu

D_MODEL = 1024
CHUNK_SHIFT = 6
HEADS = 4
HEAD_DIM = 64
V_DIM = 2 * HEAD_DIM
LANES = 128
BF16_ROWS = 16
VX_DIM = V_DIM + BF16_ROWS
QK_WIDTH = 2 * HEADS * HEAD_DIM
ATT_WIDTH = HEADS * V_DIM
CONV_WIDTH = D_MODEL // 2
CONV_K = 3
ROPE_THETA = 500000.0
ROPE_DIM = HEAD_DIM // 4
ROPE_HALF = ROPE_DIM // 2
NORM_EPS = 1e-6
SUBLN_EPS = 1e-5
REST_WIDTH = ATT_WIDTH + 4 * CONV_WIDTH + 2 * D_MODEL
QKV_WIDTH = 2 * QK_WIDTH + ATT_WIDTH
WPREP_COLS = 512

TM = 512
TMP = 1024
TQ = 512
TK = 512
FREE_TILE_GROUPS = (2, 1)
TMO = 1024
OUT_ROWS = 256

VMEM_LIMIT = 56 * 1024 * 1024
NEG = float(jnp.finfo(jnp.float32).min) / 2

_NT = (((1,), (1,)), ((), ()))


def _resident(block_shape, index_map):
    return pl.BlockSpec(block_shape, index_map, pipeline_mode=pl.Buffered(1))


def _wprep_kernel(w_ref, wT_ref, wrest_ref, *, n_transposed):
    c = pl.program_id(0)

    @pl.when(c < n_transposed)
    def _():
        wT_ref[...] = w_ref[...].T.astype(jnp.bfloat16)

    @pl.when(c >= n_transposed)
    def _():
        wrest_ref[...] = w_ref[...].astype(jnp.bfloat16)


def _proj_kernel(x_ref, pos_ref, freq_ref, prew_ref, wT_ref, wrest_ref, mb_ref, cw_ref,
                 qT_ref, k_ref, vT_ref, sz_ref, cvs_ref, gate_ref, cu_prev_ref):
    f32 = jnp.float32
    cw = CONV_WIDTH

    @pl.when(pl.program_id(1) == 0)
    def _():
        cu_prev_ref[...] = jnp.zeros(cu_prev_ref.shape, f32)

    def sigmoid(v):
        return 0.5 * jnp.tanh(0.5 * v) + 0.5

    def chain(ci, prev):
        rows = slice(ci * TM, (ci + 1) * TM)
        x = x_ref[rows, :]
        ms = jnp.mean(x * x, axis=-1, keepdims=True)
        h = (x * lax.rsqrt(ms + NORM_EPS)) * prew_ref[...]
        hb = h.astype(jnp.bfloat16)

        def cols(c0, width):
            return jnp.dot(hb, wrest_ref[:, c0:c0 + width], preferred_element_type=f32)

        c0 = ATT_WIDTH + 4 * cw
        for c in range(0, 2 * D_MODEL, 512):
            bias = mb_ref[c // D_MODEL:c // D_MODEL + 1, c % D_MODEL:c % D_MODEL + 512]
            gate_ref[rows, c:c + 512] = sigmoid(cols(c0 + c, 512) + bias).astype(jnp.bfloat16)

        z_att = cols(0, ATT_WIDTH)
        sz_ref[rows, :] = (z_att * sigmoid(z_att)).astype(jnp.bfloat16)

        c0 = ATT_WIDTH
        gb = cols(c0, cw)
        cu = cols(c0 + cw, cw) * cols(c0 + 2 * cw, cw)
        z_conv = cols(c0 + 3 * cw, cw)

        h1 = prev[7:8]
        h2 = prev[6:7]
        row = lax.broadcasted_iota(jnp.int32, (8, cw), 0)

        def shifted(by, fix_top):
            r = pltpu.roll(cu, by, 0)
            return jnp.concatenate([fix_top(r[0:8]), r[8:]], axis=0)

        r1 = shifted(1, lambda top: jnp.where(row == 0, h1, top))
        r2 = shifted(2, lambda top: jnp.where(row == 0, h2, jnp.where(row == 1, h1, top)))
        conv = cw_ref[0:1, :] * r2 + cw_ref[1:2, :] * r1 + cw_ref[2:3, :] * cu
        cvs_ref[rows, :] = ((gb * conv) * (z_conv * sigmoid(z_conv))).astype(jnp.bfloat16)

        qkT = lax.dot_general(wT_ref[0:2 * QK_WIDTH, :], hb, _NT, preferred_element_type=f32)

        pos = pos_ref[0, :, rows].astype(jnp.float32)
        ang = freq_ref[...] * pos
        cos = jnp.cos(ang)
        sin = jnp.sin(ang)

        def rot(blk):
            t1 = blk[0:ROPE_HALF]
            t2 = blk[ROPE_HALF:ROPE_DIM]
            return jnp.concatenate(
                [t1 * cos - t2 * sin, t2 * cos + t1 * sin, blk[ROPE_DIM:]], axis=0)

        scale = math.log2(math.e) / math.sqrt(HEAD_DIM)
        for hd in range(2 * HEADS):
            r0 = hd * HEAD_DIM
            qT_ref[0, r0:r0 + HEAD_DIM, rows] = (
                rot(qkT[r0:r0 + HEAD_DIM]) * scale).astype(jnp.bfloat16)
        kT = jnp.concatenate(
            [rot(qkT[QK_WIDTH + hd * HEAD_DIM:QK_WIDTH + (hd + 1) * HEAD_DIM])
             for hd in range(2 * HEADS)], axis=0)
        k = kT.T.astype(jnp.bfloat16)
        for p in range(HEADS):
            k_ref[0, p, rows, :] = k[:, p * V_DIM:(p + 1) * V_DIM]

        vT = lax.dot_general(wT_ref[2 * QK_WIDTH:QKV_WIDTH, :], hb, _NT,
                             preferred_element_type=f32)
        one_row = lax.broadcasted_iota(jnp.int32, (BF16_ROWS, TM), 0) == 0
        ones_rows = jnp.where(one_row, 1.0, 0.0).astype(jnp.bfloat16)
        for p in range(HEADS):
            vT_ref[0, ci, p * VX_DIM:p * VX_DIM + V_DIM, :] = (
                vT[p * V_DIM:(p + 1) * V_DIM].astype(jnp.bfloat16))
            vT_ref[0, ci, p * VX_DIM + V_DIM:(p + 1) * VX_DIM, :] = ones_rows
        return cu[TM - 8:TM]

    prev = cu_prev_ref[...]
    for ci in range(TMP // TM):
        prev = chain(ci, prev)
    cu_prev_ref[...] = prev


def _attn_kernel(klist_ref, nunm_ref, ntot_ref,
                 qT_ref, k_ref, vT_ref, posr_ref, posk_ref,
                 lq1_ref, lk1_ref, lq2_ref, lk2_ref, sw_ref, sz_ref,
                 o_ref, q2_ref, acc_ref, m_ref, sa_ref, sb_ref, mt0_ref,
                 *, nk, n_tiles, lambda_init):
    s = pl.program_id(0)
    active = s < n_tiles
    qt = jnp.minimum(s, n_tiles - 1)
    n = jnp.where(active, ntot_ref[qt], 0)
    n_unm = jnp.where(active, nunm_ref[qt], 0)

    @pl.when(s == 0)
    def _():
        acc_ref[...] = jnp.ones(acc_ref.shape, jnp.float32)

    qc = lax.shift_right_arithmetic(posr_ref[0], CHUNK_SHIFT)
    qc2 = jnp.concatenate([qc, qc], axis=1)

    def init_state():
        for p in range(HEADS):
            q2_ref[p] = jnp.zeros(q2_ref.shape[1:], q2_ref.dtype)
            q2_ref[p, 0:HEAD_DIM, 0:TQ] = qT_ref[0, p * V_DIM:p * V_DIM + HEAD_DIM, :]
            q2_ref[p, HEAD_DIM:V_DIM, TQ:2 * TQ] = (
                qT_ref[0, p * V_DIM + HEAD_DIM:(p + 1) * V_DIM, :])
            m_ref[p] = jnp.full(m_ref.shape[1:], NEG, jnp.float32)
            acc_ref[p] = jnp.zeros(acc_ref.shape[1:], jnp.float32)

    def key_tile(r):
        return klist_ref[qt * nk + r]

    def score(p, j, s_ref, masked):
        k0 = pl.multiple_of(j * TK, TK)
        s = jnp.dot(k_ref[0, p, pl.ds(k0, TK), :], q2_ref[p],
                    preferred_element_type=jnp.float32)
        if masked:
            kc = lax.shift_right_arithmetic(posk_ref[0, j], CHUNK_SHIFT)
            kc = jnp.broadcast_to(kc, (LANES, TK)).T
            kc = jnp.concatenate([kc] * (2 * TQ // LANES), axis=1)
            s = jnp.where(kc <= qc2, s, NEG)
        s_ref[...] = s
        return jnp.max(s, axis=0, keepdims=True)

    def update(p, j, s_ref, mt):
        m_old = m_ref[p]
        m_new = jnp.maximum(m_old, mt)
        alpha = jnp.exp2(m_old - m_new)
        pr = jnp.exp2(s_ref[...] - m_new).astype(jnp.bfloat16)
        vt = vT_ref[0, j, p * VX_DIM:(p + 1) * VX_DIM, :]
        acc_ref[p] = alpha * acc_ref[p] + jnp.dot(vt, pr, preferred_element_type=jnp.float32)
        m_ref[p] = m_new

    def block(j, j_next, masked, next_masked):
        mt1 = score(1, j, sb_ref, masked)
        update(0, j, sa_ref, mt0_ref[...])
        mt2 = score(2, j, sa_ref, masked)
        update(1, j, sb_ref, mt1)
        mt3 = score(3, j, sb_ref, masked)
        update(2, j, sa_ref, mt2)
        if next_masked is not None:
            mt0_ref[...] = score(0, j_next, sa_ref, next_masked)
        update(3, j, sb_ref, mt3)

    lam = (jnp.exp(jnp.sum(lq1_ref[...] * lk1_ref[...], axis=-1, keepdims=True))
           - jnp.exp(jnp.sum(lq2_ref[...] * lk2_ref[...], axis=-1, keepdims=True))
           + lambda_init)

    def finish_previous():
        for p in range(HEADS):
            o = acc_ref[p, 0:V_DIM, :] / acc_ref[p, V_DIM:V_DIM + 1, :]
            d = o[:, :TQ] - lam * o[:, TQ:]
            ms = jnp.mean(d * d, axis=0, keepdims=True)
            dn = ((d * lax.rsqrt(ms + SUBLN_EPS)) * sw_ref[...]) * (1.0 - lambda_init)
            lanes = slice(p * V_DIM, (p + 1) * V_DIM)
            o_ref[:, lanes] = (dn.T * sz_ref[:, lanes].astype(jnp.float32)).astype(o_ref.dtype)

    @pl.when(n_unm > 0)
    def _():
        finish_previous()
        init_state()
        mt0_ref[...] = score(0, key_tile(0), sa_ref, False)

    @pl.when(active & (n_unm == 0))
    def _():
        finish_previous()
        init_state()
        mt0_ref[...] = score(0, key_tile(0), sa_ref, True)

    @pl.when(jnp.logical_not(active))
    def _():
        finish_previous()

    done = 0
    for group in FREE_TILE_GROUPS:
        n_groups = jnp.maximum(n_unm - 1 - done, 0) // group

        def free_blocks(g, carry, group=group, done=done):
            r0 = done + g * group
            for r in range(group):
                block(key_tile(r0 + r), key_tile(r0 + r + 1), False, False)
            return carry

        lax.fori_loop(0, n_groups, free_blocks, 0)
        done = done + n_groups * group

    @pl.when((n_unm > 0) & (n > n_unm))
    def _():
        block(key_tile(n_unm - 1), key_tile(n_unm), False, True)

    @pl.when((n_unm > 0) & (n == n_unm))
    def _():
        block(key_tile(n_unm - 1), None, False, None)

    def masked_block(r, carry):
        block(key_tile(r), key_tile(r + 1), True, True)
        return carry

    lax.fori_loop(n_unm, n - 1, masked_block, 0)

    @pl.when(n > n_unm)
    def _():
        block(key_tile(n - 1), None, True, None)


def _out_kernel(x_ref, att_ref, cvs_ref, gate_ref, wao_ref, wco_ref, wo_ref, postw_ref, o_ref):
    f32 = jnp.float32
    for r0 in range(0, TMO, OUT_ROWS):
        rows = slice(r0, r0 + OUT_ROWS)
        y_att = jnp.dot(att_ref[rows, :], wao_ref[...], preferred_element_type=f32)
        y_conv = jnp.dot(cvs_ref[rows, :], wco_ref[...], preferred_element_type=f32)
        m = (gate_ref[rows, 0:D_MODEL].astype(f32) * y_att
             + gate_ref[rows, D_MODEL:2 * D_MODEL].astype(f32) * y_conv)
        o = jnp.dot(m.astype(jnp.bfloat16), wo_ref[...], preferred_element_type=f32)
        ms = jnp.mean(o * o, axis=-1, keepdims=True)
        o_ref[rows, :] = x_ref[rows, :] + (o * lax.rsqrt(ms + NORM_EPS)) * postw_ref[...]


def _key_tile_lists(positions, batch, nq, nk):
    chunk = lax.shift_right_arithmetic(positions, CHUNK_SHIFT)
    kch = chunk.reshape(batch, 1, nk, TK)
    qch = chunk.reshape(batch, nq, 1, TQ)
    needed = kch.min(-1) <= qch.max(-1)
    nomask = needed & (kch.max(-1) <= qch.min(-1))
    rank = jnp.where(nomask, 0, jnp.where(needed, 1, 2))
    order = jnp.argsort(rank, axis=-1, stable=True).astype(jnp.int32)

    def count(flag):
        return flag.sum(-1).astype(jnp.int32).reshape(-1)

    return order.reshape(-1), count(nomask), count(needed)


def _layer(x2, positions, pre_w, w_in, merge_bias, lq1, lk1, lq2, lk2, subln_w,
           w_att_out, conv_w, w_conv_out, w_out, post_w, lambda_init, batch, seq):
    bf16 = jnp.bfloat16
    n = batch * seq
    nt = seq // TMP
    nq = seq // TQ
    nk = seq // TK
    assert seq % TMP == 0 and seq % TQ == 0 and TM == TK and n % TMO == 0

    freqs = (ROPE_THETA ** (-jnp.arange(0, ROPE_DIM, 2, dtype=jnp.float32) / ROPE_DIM)
             ).reshape(ROPE_HALF, 1)
    n_t = QKV_WIDTH // WPREP_COLS
    wT, wrest = pl.pallas_call(
        functools.partial(_wprep_kernel, n_transposed=n_t),
        grid=((QKV_WIDTH + REST_WIDTH) // WPREP_COLS,),
        in_specs=[pl.BlockSpec((D_MODEL, WPREP_COLS), lambda c: (0, c))],
        out_specs=[
            pl.BlockSpec((WPREP_COLS, D_MODEL), lambda c: (jnp.minimum(c, n_t - 1), 0)),
            pl.BlockSpec((D_MODEL, WPREP_COLS), lambda c: (0, jnp.maximum(c - n_t, 0))),
        ],
        out_shape=[
            jax.ShapeDtypeStruct((QKV_WIDTH, D_MODEL), bf16),
            jax.ShapeDtypeStruct((D_MODEL, REST_WIDTH), bf16),
        ],
        compiler_params=pltpu.CompilerParams(dimension_semantics=("arbitrary",)),
        name="wprep",
    )(w_in)
    pos_row = positions.reshape(batch, 1, seq)
    pos_tiles = positions.reshape(batch, nk, 1, TK)

    cparams = functools.partial(pltpu.CompilerParams, vmem_limit_bytes=VMEM_LIMIT)

    def rows(width):
        return pl.BlockSpec((TMP, width), lambda b, t: (b * nt + t, 0))

    qT, k, vT, sz, cvs, gates = pl.pallas_call(
        _proj_kernel,
        grid=(batch, nt),
        in_specs=[
            rows(D_MODEL),
            pl.BlockSpec((1, 1, TMP), lambda b, t: (b, 0, t)),
            _resident((ROPE_HALF, 1), lambda b, t: (0, 0)),
            _resident((1, D_MODEL), lambda b, t: (0, 0)),
            _resident((QKV_WIDTH, D_MODEL), lambda b, t: (0, 0)),
            _resident((D_MODEL, REST_WIDTH), lambda b, t: (0, 0)),
            _resident((2, D_MODEL), lambda b, t: (0, 0)),
            _resident((CONV_K, CONV_WIDTH), lambda b, t: (0, 0)),
        ],
        out_specs=[
            pl.BlockSpec((1, QK_WIDTH, TMP), lambda b, t: (b, 0, t)),
            pl.BlockSpec((1, HEADS, TMP, V_DIM), lambda b, t: (b, 0, t, 0)),
            pl.BlockSpec((1, TMP // TK, HEADS * VX_DIM, TK), lambda b, t: (b, t, 0, 0)),
            rows(ATT_WIDTH),
            rows(CONV_WIDTH),
            rows(2 * D_MODEL),
        ],
        out_shape=[
            jax.ShapeDtypeStruct((batch, QK_WIDTH, seq), bf16),
            jax.ShapeDtypeStruct((batch, HEADS, seq, V_DIM), bf16),
            jax.ShapeDtypeStruct((batch, nk, HEADS * VX_DIM, TK), bf16),
            jax.ShapeDtypeStruct((n, ATT_WIDTH), bf16),
            jax.ShapeDtypeStruct((n, CONV_WIDTH), bf16),
            jax.ShapeDtypeStruct((n, 2 * D_MODEL), bf16),
        ],
        scratch_shapes=[pltpu.VMEM((8, CONV_WIDTH), jnp.float32)],
        compiler_params=cparams(dimension_semantics=("arbitrary", "arbitrary")),
        name="proj",
    )(x2, pos_row, freqs, pre_w.reshape(1, D_MODEL), wT, wrest,
      merge_bias, conv_w.reshape(CONV_K, CONV_WIDTH))

    klist, n_unm, n_tot = _key_tile_lists(positions, batch, nq, nk)

    n_tiles = batch * nq

    def cur(s):
        t = jnp.minimum(s, n_tiles - 1)
        return t // nq, t % nq

    def prev(s):
        return jnp.maximum(s - 1, 0)

    att = pl.pallas_call(
        functools.partial(_attn_kernel, nk=nk, n_tiles=n_tiles, lambda_init=lambda_init),
        grid_spec=pltpu.PrefetchScalarGridSpec(
            num_scalar_prefetch=3,
            grid=(n_tiles + 1,),
            in_specs=[
                pl.BlockSpec((1, QK_WIDTH, TQ), lambda s, *_: (cur(s)[0], 0, cur(s)[1])),
                pl.BlockSpec((1, HEADS, seq, V_DIM), lambda s, *_: (cur(s)[0], 0, 0, 0)),
                pl.BlockSpec((1, nk, HEADS * VX_DIM, TK), lambda s, *_: (cur(s)[0], 0, 0, 0)),
                pl.BlockSpec((1, 1, TQ), lambda s, *_: (cur(s)[0], 0, cur(s)[1])),
                pl.BlockSpec((1, nk, 1, TK), lambda s, *_: (cur(s)[0], 0, 0, 0)),
                _resident((1, HEAD_DIM), lambda s, *_: (0, 0)),
                _resident((1, HEAD_DIM), lambda s, *_: (0, 0)),
                _resident((1, HEAD_DIM), lambda s, *_: (0, 0)),
                _resident((1, HEAD_DIM), lambda s, *_: (0, 0)),
                _resident((V_DIM, 1), lambda s, *_: (0, 0)),
                pl.BlockSpec((TQ, ATT_WIDTH), lambda s, *_: (prev(s), 0)),
            ],
            out_specs=pl.BlockSpec((TQ, ATT_WIDTH), lambda s, *_: (prev(s), 0)),
            scratch_shapes=[
                pltpu.VMEM((HEADS, V_DIM, 2 * TQ), bf16),
                pltpu.VMEM((HEADS, VX_DIM, 2 * TQ), jnp.float32),
                pltpu.VMEM((HEADS, 1, 2 * TQ), jnp.float32),
                pltpu.VMEM((TK, 2 * TQ), jnp.float32),
                pltpu.VMEM((TK, 2 * TQ), jnp.float32),
                pltpu.VMEM((1, 2 * TQ), jnp.float32),
            ],
        ),
        out_shape=jax.ShapeDtypeStruct((n, ATT_WIDTH), bf16),
        compiler_params=cparams(dimension_semantics=("arbitrary",)),
        name="attn",
    )(klist, n_unm, n_tot, qT, k, vT, pos_row, pos_tiles,
      lq1.reshape(1, HEAD_DIM), lk1.reshape(1, HEAD_DIM),
      lq2.reshape(1, HEAD_DIM), lk2.reshape(1, HEAD_DIM), subln_w.reshape(V_DIM, 1), sz)

    out = pl.pallas_call(
        _out_kernel,
        grid=(n // TMO,),
        in_specs=[
            pl.BlockSpec((TMO, D_MODEL), lambda t: (t, 0)),
            pl.BlockSpec((TMO, ATT_WIDTH), lambda t: (t, 0)),
            pl.BlockSpec((TMO, CONV_WIDTH), lambda t: (t, 0)),
            pl.BlockSpec((TMO, 2 * D_MODEL), lambda t: (t, 0)),
            _resident((ATT_WIDTH, D_MODEL), lambda t: (0, 0)),
            _resident((CONV_WIDTH, D_MODEL), lambda t: (0, 0)),
            _resident((D_MODEL, D_MODEL), lambda t: (0, 0)),
            _resident((1, D_MODEL), lambda t: (0, 0)),
        ],
        out_specs=pl.BlockSpec((TMO, D_MODEL), lambda t: (t, 0)),
        out_shape=jax.ShapeDtypeStruct((n, D_MODEL), jnp.float32),
        compiler_params=cparams(dimension_semantics=("arbitrary",)),
        name="outp",
    )(x2, att, cvs, gates, w_att_out.astype(bf16), w_conv_out.astype(bf16), w_out.astype(bf16),
      post_w.reshape(1, D_MODEL))
    return out


def kernel(x, positions, pre_norm_w, w_in, merge_bias, lambda_q1, lambda_k1, lambda_q2,
           lambda_k2, subln_w, w_att_out, conv_w, w_conv_out, w_out, post_norm_w):
    batch, seq, _ = x.shape
    depth = w_in.shape[0]
    x2 = x.reshape(batch * seq, D_MODEL)
    for layer in range(depth):
        lambda_init = 0.8 - 0.6 * math.exp(-0.3 * layer)
        x2 = _layer(x2, positions, pre_norm_w[layer], w_in[layer], merge_bias[layer],
                    lambda_q1[layer], lambda_k1[layer], lambda_q2[layer], lambda_k2[layer],
                    subln_w[layer], w_att_out[layer], conv_w[layer], w_conv_out[layer],
                    w_out[layer], post_norm_w[layer], lambda_init, batch, seq)
    return x2.reshape(batch, seq, D_MODEL)
```

```python
import functools
import math

import jax
import jax.numpy as jnp
from jax import lax
from jax.experimental import pallas as pl
from jax.experimental.pallas import tpu as pltpu

D_MODEL = 1024
CHUNK_SHIFT = 6
HEADS = 4
HEAD_DIM = 64
V_DIM = 2 * HEAD_DIM
LANES = 128
BF16_ROWS = 16
VX_DIM = V_DIM + BF16_ROWS
QK_WIDTH = 2 * HEADS * HEAD_DIM
ATT_WIDTH = HEADS * V_DIM
CONV_WIDTH = D_MODEL // 2
CONV_K = 3
ROPE_THETA = 500000.0
ROPE_DIM = HEAD_DIM // 4
ROPE_HALF = ROPE_DIM // 2
NORM_EPS = 1e-6
SUBLN_EPS = 1e-5
REST_WIDTH = ATT_WIDTH + 4 * CONV_WIDTH + 2 * D_MODEL
QKV_WIDTH = 2 * QK_WIDTH + ATT_WIDTH
WPREP_COLS = 512

TM = 512
TMP = 1024
TQ = 512
TK = 512
FREE_TILE_GROUPS = (2, 1)
TMO = 1024
OUT_ROWS = 256

VMEM_LIMIT = 56 * 1024 * 1024
NEG = float(jnp.finfo(jnp.float32).min) / 2

_NT = (((1,), (1,)), ((), ()))


def _resident(block_shape, index_map):
    return pl.BlockSpec(block_shape, index_map, pipeline_mode=pl.Buffered(1))


def _wprep_kernel(w_ref, wT_ref, wrest_ref, *, n_transposed):
    c = pl.program_id(0)

    @pl.when(c < n_transposed)
    def _():
        wT_ref[...] = w_ref[...].T.astype(jnp.bfloat16)

    @pl.when(c >= n_transposed)
    def _():
        wrest_ref[...] = w_ref[...].astype(jnp.bfloat16)


def _proj_kernel(x_ref, pos_ref, freq_ref, prew_ref, wT_ref, wrest_ref, mb_ref, cw_ref,
                 qT_ref, k_ref, vT_ref, sz_ref, cvs_ref, gate_ref, cu_prev_ref):
    f32 = jnp.float32
    cw = CONV_WIDTH

    @pl.when(pl.program_id(1) == 0)
    def _():
        cu_prev_ref[...] = jnp.zeros(cu_prev_ref.shape, f32)

    def sigmoid(v):
        return 0.5 * jnp.tanh(0.5 * v) + 0.5

    def chain(ci, prev):
        rows = slice(ci * TM, (ci + 1) * TM)
        x = x_ref[rows, :]
        ms = jnp.mean(x * x, axis=-1, keepdims=True)
        h = (x * lax.rsqrt(ms + NORM_EPS)) * prew_ref[...]
        hb = h.astype(jnp.bfloat16)

        def cols(c0, width):
            return jnp.dot(hb, wrest_ref[:, c0:c0 + width], preferred_element_type=f32)

        c0 = ATT_WIDTH + 4 * cw
        for c in range(0, 2 * D_MODEL, 512):
            bias = mb_ref[c // D_MODEL:c // D_MODEL + 1, c % D_MODEL:c % D_MODEL + 512]
            gate_ref[rows, c:c + 512] = sigmoid(cols(c0 + c, 512) + bias).astype(jnp.bfloat16)

        z_att = cols(0, ATT_WIDTH)
        sz_ref[rows, :] = (z_att * sigmoid(z_att)).astype(jnp.bfloat16)

        c0 = ATT_WIDTH
        gb = cols(c0, cw)
        cu = cols(c0 + cw, cw) * cols(c0 + 2 * cw, cw)
        z_conv = cols(c0 + 3 * cw, cw)

        h1 = prev[7:8]
        h2 = prev[6:7]
        row = lax.broadcasted_iota(jnp.int32, (8, cw), 0)

        def shifted(by, fix_top):
            r = pltpu.roll(cu, by, 0)
            return jnp.concatenate([fix_top(r[0:8]), r[8:]], axis=0)

        r1 = shifted(1, lambda top: jnp.where(row == 0, h1, top))
        r2 = shifted(2, lambda top: jnp.where(row == 0, h2, jnp.where(row == 1, h1, top)))
        conv = cw_ref[0:1, :] * r2 + cw_ref[1:2, :] * r1 + cw_ref[2:3, :] * cu
        cvs_ref[rows, :] = ((gb * conv) * (z_conv * sigmoid(z_conv))).astype(jnp.bfloat16)

        qkT = lax.dot_general(wT_ref[0:2 * QK_WIDTH, :], hb, _NT, preferred_element_type=f32)

        pos = pos_ref[0, :, rows].astype(jnp.float32)
        ang = freq_ref[...] * pos
        cos = jnp.cos(ang)
        sin = jnp.sin(ang)

        def rot(blk):
            t1 = blk[0:ROPE_HALF]
            t2 = blk[ROPE_HALF:ROPE_DIM]
            return jnp.concatenate(
                [t1 * cos - t2 * sin, t2 * cos + t1 * sin, blk[ROPE_DIM:]], axis=0)

        scale = math.log2(math.e) / math.sqrt(HEAD_DIM)
        for hd in range(2 * HEADS):
            r0 = hd * HEAD_DIM
            qT_ref[0, r0:r0 + HEAD_DIM, rows] = (
                rot(qkT[r0:r0 + HEAD_DIM]) * scale).astype(jnp.bfloat16)
        kT = jnp.concatenate(
            [rot(qkT[QK_WIDTH + hd * HEAD_DIM:QK_WIDTH + (hd + 1) * HEAD_DIM])
             for hd in range(2 * HEADS)], axis=0)
        k = kT.T.astype(jnp.bfloat16)
        for p in range(HEADS):
            k_ref[0, p, rows, :] = k[:, p * V_DIM:(p + 1) * V_DIM]

        vT = lax.dot_general(wT_ref[2 * QK_WIDTH:QKV_WIDTH, :], hb, _NT,
                             preferred_element_type=f32)
        one_row = lax.broadcasted_iota(jnp.int32, (BF16_ROWS, TM), 0) == 0
        ones_rows = jnp.where(one_row, 1.0, 0.0).astype(jnp.bfloat16)
        for p in range(HEADS):
            vT_ref[0, ci, p * VX_DIM:p * VX_DIM + V_DIM, :] = (
                vT[p * V_DIM:(p + 1) * V_DIM].astype(jnp.bfloat16))
            vT_ref[0, ci, p * VX_DIM + V_DIM:(p + 1) * VX_DIM, :] = ones_rows
        return cu[TM - 8:TM]

    prev = cu_prev_ref[...]
    for ci in range(TMP // TM):
        prev = chain(ci, prev)
    cu_prev_ref[...] = prev


def _attn_kernel(klist_ref, nunm_ref, ntot_ref,
                 qT_ref, k_ref, vT_ref, posr_ref, posk_ref,
                 lq1_ref, lk1_ref, lq2_ref, lk2_ref, sw_ref, sz_ref,
                 o_ref, q2_ref, acc_ref, m_ref, sa_ref, sb_ref, mt0_ref,
                 *, nq, nk, lambda_init):
    qt = pl.program_id(0) * nq + pl.program_id(1)
    n = ntot_ref[qt]
    n_unm = nunm_ref[qt]

    qc = lax.shift_right_arithmetic(posr_ref[0], CHUNK_SHIFT)
    qc2 = jnp.concatenate([qc, qc], axis=1)

    def init_state():
        for p in range(HEADS):
            q2_ref[p] = jnp.zeros(q2_ref.shape[1:], q2_ref.dtype)
            q2_ref[p, 0:HEAD_DIM, 0:TQ] = qT_ref[0, p * V_DIM:p * V_DIM + HEAD_DIM, :]
            q2_ref[p, HEAD_DIM:V_DIM, TQ:2 * TQ] = (
                qT_ref[0, p * V_DIM + HEAD_DIM:(p + 1) * V_DIM, :])
            m_ref[p] = jnp.full(m_ref.shape[1:], NEG, jnp.float32)
            acc_ref[p] = jnp.zeros(acc_ref.shape[1:], jnp.float32)

    def key_tile(r):
        return klist_ref[qt * nk + r]

    def score(p, j, s_ref, masked):
        k0 = pl.multiple_of(j * TK, TK)
        s = jnp.dot(k_ref[0, p, pl.ds(k0, TK), :], q2_ref[p],
                    preferred_element_type=jnp.float32)
        if masked:
            kc = lax.shift_right_arithmetic(posk_ref[0, j], CHUNK_SHIFT)
            kc = jnp.broadcast_to(kc, (LANES, TK)).T
            kc = jnp.concatenate([kc] * (2 * TQ // LANES), axis=1)
            s = jnp.where(kc <= qc2, s, NEG)
        s_ref[...] = s
        return jnp.max(s, axis=0, keepdims=True)

    def update(p, j, s_ref, mt):
        m_old = m_ref[p]
        m_new = jnp.maximum(m_old, mt)
        alpha = jnp.exp2(m_old - m_new)
        pr = jnp.exp2(s_ref[...] - m_new).astype(jnp.bfloat16)
        vt = vT_ref[0, j, p * VX_DIM:(p + 1) * VX_DIM, :]
        acc_ref[p] = alpha * acc_ref[p] + jnp.dot(vt, pr, preferred_element_type=jnp.float32)
        m_ref[p] = m_new

    def block(j, j_next, masked, next_masked):
        mt1 = score(1, j, sb_ref, masked)
        update(0, j, sa_ref, mt0_ref[...])
        mt2 = score(2, j, sa_ref, masked)
        update(1, j, sb_ref, mt1)
        mt3 = score(3, j, sb_ref, masked)
        update(2, j, sa_ref, mt2)
        if next_masked is not None:
            mt0_ref[...] = score(0, j_next, sa_ref, next_masked)
        update(3, j, sb_ref, mt3)

    @pl.when(n_unm > 0)
    def _():
        init_state()
        mt0_ref[...] = score(0, key_tile(0), sa_ref, False)

    @pl.when(n_unm == 0)
    def _():
        init_state()
        mt0_ref[...] = score(0, key_tile(0), sa_ref, True)

    done = 0
    for group in FREE_TILE_GROUPS:
        n_groups = jnp.maximum(n_unm - 1 - done, 0) // group

        def free_blocks(g, carry, group=group, done=done):
            r0 = done + g * group
            for r in range(group):
                block(key_tile(r0 + r), key_tile(r0 + r + 1), False, False)
            return carry

        lax.fori_loop(0, n_groups, free_blocks, 0)
        done = done + n_groups * group

    @pl.when((n_unm > 0) & (n > n_unm))
    def _():
        block(key_tile(n_unm - 1), key_tile(n_unm), False, True)

    @pl.when((n_unm > 0) & (n == n_unm))
    def _():
        block(key_tile(n_unm - 1), None, False, None)

    def masked_block(r, carry):
        block(key_tile(r), key_tile(r + 1), True, True)
        return carry

    lax.fori_loop(n_unm, n - 1, masked_block, 0)

    @pl.when(n > n_unm)
    def _():
        block(key_tile(n - 1), None, True, None)

    lam = (jnp.exp(jnp.sum(lq1_ref[...] * lk1_ref[...], axis=-1, keepdims=True))
           - jnp.exp(jnp.sum(lq2_ref[...] * lk2_ref[...], axis=-1, keepdims=True))
           + lambda_init)
    for p in range(HEADS):
        o = acc_ref[p, 0:V_DIM, :] / acc_ref[p, V_DIM:V_DIM + 1, :]
        d = o[:, :TQ] - lam * o[:, TQ:]
        ms = jnp.mean(d * d, axis=0, keepdims=True)
        dn = ((d * lax.rsqrt(ms + SUBLN_EPS)) * sw_ref[...]) * (1.0 - lambda_init)
        lanes = slice(p * V_DIM, (p + 1) * V_DIM)
        o_ref[:, lanes] = (dn.T * sz_ref[:, lanes].astype(jnp.float32)).astype(o_ref.dtype)


def _out_kernel(x_ref, att_ref, cvs_ref, gate_ref, wao_ref, wco_ref, wo_ref, postw_ref, o_ref):
    f32 = jnp.float32
    for r0 in range(0, TMO, OUT_ROWS):
        rows = slice(r0, r0 + OUT_ROWS)
        y_att = jnp.dot(att_ref[rows, :], wao_ref[...], preferred_element_type=f32)
        y_conv = jnp.dot(cvs_ref[rows, :], wco_ref[...], preferred_element_type=f32)
        m = (gate_ref[rows, 0:D_MODEL].astype(f32) * y_att
             + gate_ref[rows, D_MODEL:2 * D_MODEL].astype(f32) * y_conv)
        o = jnp.dot(m.astype(jnp.bfloat16), wo_ref[...], preferred_element_type=f32)
        ms = jnp.mean(o * o, axis=-1, keepdims=True)
        o_ref[rows, :] = x_ref[rows, :] + (o * lax.rsqrt(ms + NORM_EPS)) * postw_ref[...]


def _key_tile_lists(positions, batch, nq, nk):
    chunk = lax.shift_right_arithmetic(positions, CHUNK_SHIFT)
    kch = chunk.reshape(batch, 1, nk, TK)
    qch = chunk.reshape(batch, nq, 1, TQ)
    needed = kch.min(-1) <= qch.max(-1)
    nomask = needed & (kch.max(-1) <= qch.min(-1))
    rank = jnp.where(nomask, 0, jnp.where(needed, 1, 2))
    order = jnp.argsort(rank, axis=-1, stable=True).astype(jnp.int32)

    def count(flag):
        return flag.sum(-1).astype(jnp.int32).reshape(-1)

    return order.reshape(-1), count(nomask), count(needed)


def _layer(x2, positions, pre_w, w_in, merge_bias, lq1, lk1, lq2, lk2, subln_w,
           w_att_out, conv_w, w_conv_out, w_out, post_w, lambda_init, batch, seq):
    bf16 = jnp.bfloat16
    n = batch * seq
    nt = seq // TMP
    nq = seq // TQ
    nk = seq // TK
    assert seq % TMP == 0 and seq % TQ == 0 and TM == TK and n % TMO == 0

    freqs = (ROPE_THETA ** (-jnp.arange(0, ROPE_DIM, 2, dtype=jnp.float32) / ROPE_DIM)
             ).reshape(ROPE_HALF, 1)
    n_t = QKV_WIDTH // WPREP_COLS
    wT, wrest = pl.pallas_call(
        functools.partial(_wprep_kernel, n_transposed=n_t),
        grid=((QKV_WIDTH + REST_WIDTH) // WPREP_COLS,),
        in_specs=[pl.BlockSpec((D_MODEL, WPREP_COLS), lambda c: (0, c))],
        out_specs=[
            pl.BlockSpec((WPREP_COLS, D_MODEL), lambda c: (jnp.minimum(c, n_t - 1), 0)),
            pl.BlockSpec((D_MODEL, WPREP_COLS), lambda c: (0, jnp.maximum(c - n_t, 0))),
        ],
        out_shape=[
            jax.ShapeDtypeStruct((QKV_WIDTH, D_MODEL), bf16),
            jax.ShapeDtypeStruct((D_MODEL, REST_WIDTH), bf16),
        ],
        compiler_params=pltpu.CompilerParams(dimension_semantics=("arbitrary",)),
        name="wprep",
    )(w_in)
    pos_row = positions.reshape(batch, 1, seq)
    pos_tiles = positions.reshape(batch, nk, 1, TK)

    cparams = functools.partial(pltpu.CompilerParams, vmem_limit_bytes=VMEM_LIMIT)

    def rows(width):
        return pl.BlockSpec((TMP, width), lambda b, t: (b * nt + t, 0))

    qT, k, vT, sz, cvs, gates = pl.pallas_call(
        _proj_kernel,
        grid=(batch, nt),
        in_specs=[
            rows(D_MODEL),
            pl.BlockSpec((1, 1, TMP), lambda b, t: (b, 0, t)),
            _resident((ROPE_HALF, 1), lambda b, t: (0, 0)),
            _resident((1, D_MODEL), lambda b, t: (0, 0)),
            _resident((QKV_WIDTH, D_MODEL), lambda b, t: (0, 0)),
            _resident((D_MODEL, REST_WIDTH), lambda b, t: (0, 0)),
            _resident((2, D_MODEL), lambda b, t: (0, 0)),
            _resident((CONV_K, CONV_WIDTH), lambda b, t: (0, 0)),
        ],
        out_specs=[
            pl.BlockSpec((1, QK_WIDTH, TMP), lambda b, t: (b, 0, t)),
            pl.BlockSpec((1, HEADS, TMP, V_DIM), lambda b, t: (b, 0, t, 0)),
            pl.BlockSpec((1, TMP // TK, HEADS * VX_DIM, TK), lambda b, t: (b, t, 0, 0)),
            rows(ATT_WIDTH),
            rows(CONV_WIDTH),
            rows(2 * D_MODEL),
        ],
        out_shape=[
            jax.ShapeDtypeStruct((batch, QK_WIDTH, seq), bf16),
            jax.ShapeDtypeStruct((batch, HEADS, seq, V_DIM), bf16),
            jax.ShapeDtypeStruct((batch, nk, HEADS * VX_DIM, TK), bf16),
            jax.ShapeDtypeStruct((n, ATT_WIDTH), bf16),
            jax.ShapeDtypeStruct((n, CONV_WIDTH), bf16),
            jax.ShapeDtypeStruct((n, 2 * D_MODEL), bf16),
        ],
        scratch_shapes=[pltpu.VMEM((8, CONV_WIDTH), jnp.float32)],
        compiler_params=cparams(dimension_semantics=("arbitrary", "arbitrary")),
        name="proj",
    )(x2, pos_row, freqs, pre_w.reshape(1, D_MODEL), wT, wrest,
      merge_bias, conv_w.reshape(CONV_K, CONV_WIDTH))

    klist, n_unm, n_tot = _key_tile_lists(positions, batch, nq, nk)

    att = pl.pallas_call(
        functools.partial(_attn_kernel, nq=nq, nk=nk, lambda_init=lambda_init),
        grid_spec=pltpu.PrefetchScalarGridSpec(
            num_scalar_prefetch=3,
            grid=(batch, nq),
            in_specs=[
                pl.BlockSpec((1, QK_WIDTH, TQ), lambda b, i, *_: (b, 0, i)),
                pl.BlockSpec((1, HEADS, seq, V_DIM), lambda b, i, *_: (b, 0, 0, 0)),
                pl.BlockSpec((1, nk, HEADS * VX_DIM, TK), lambda b, i, *_: (b, 0, 0, 0)),
                pl.BlockSpec((1, 1, TQ), lambda b, i, *_: (b, 0, i)),
                pl.BlockSpec((1, nk, 1, TK), lambda b, i, *_: (b, 0, 0, 0)),
                _resident((1, HEAD_DIM), lambda b, i, *_: (0, 0)),
                _resident((1, HEAD_DIM), lambda b, i, *_: (0, 0)),
                _resident((1, HEAD_DIM), lambda b, i, *_: (0, 0)),
                _resident((1, HEAD_DIM), lambda b, i, *_: (0, 0)),
                _resident((V_DIM, 1), lambda b, i, *_: (0, 0)),
                pl.BlockSpec((TQ, ATT_WIDTH), lambda b, i, *_: (b * nq + i, 0)),
            ],
            out_specs=pl.BlockSpec((TQ, ATT_WIDTH), lambda b, i, *_: (b * nq + i, 0)),
            scratch_shapes=[
                pltpu.VMEM((HEADS, V_DIM, 2 * TQ), bf16),
                pltpu.VMEM((HEADS, VX_DIM, 2 * TQ), jnp.float32),
                pltpu.VMEM((HEADS, 1, 2 * TQ), jnp.float32),
                pltpu.VMEM((TK, 2 * TQ), jnp.float32),
                pltpu.VMEM((TK, 2 * TQ), jnp.float32),
                pltpu.VMEM((1, 2 * TQ), jnp.float32),
            ],
        ),
        out_shape=jax.ShapeDtypeStruct((n, ATT_WIDTH), bf16),
        compiler_params=cparams(dimension_semantics=("arbitrary", "arbitrary")),
        name="attn",
    )(klist, n_unm, n_tot, qT, k, vT, pos_row, pos_tiles,
      lq1.reshape(1, HEAD_DIM), lk1.reshape(1, HEAD_DIM),
      lq2.reshape(1, HEAD_DIM), lk2.reshape(1, HEAD_DIM), subln_w.reshape(V_DIM, 1), sz)

    out = pl.pallas_call(
        _out_kernel,
        grid=(n // TMO,),
        in_specs=[
            pl.BlockSpec((TMO, D_MODEL), lambda t: (t, 0)),
            pl.BlockSpec((TMO, ATT_WIDTH), lambda t: (t, 0)),
            pl.BlockSpec((TMO, CONV_WIDTH), lambda t: (t, 0)),
            pl.BlockSpec((TMO, 2 * D_MODEL), lambda t: (t, 0)),
            _resident((ATT_WIDTH, D_MODEL), lambda t: (0, 0)),
            _resident((CONV_WIDTH, D_MODEL), lambda t: (0, 0)),
            _resident((D_MODEL, D_MODEL), lambda t: (0, 0)),
            _resident((1, D_MODEL), lambda t: (0, 0)),
        ],
        out_specs=pl.BlockSpec((TMO, D_MODEL), lambda t: (t, 0)),
        out_shape=jax.ShapeDtypeStruct((n, D_MODEL), jnp.float32),
        compiler_params=cparams(dimension_semantics=("arbitrary",)),
        name="outp",
    )(x2, att, cvs, gates, w_att_out.astype(bf16), w_conv_out.astype(bf16), w_out.astype(bf16),
      post_w.reshape(1, D_MODEL))
    return out


def kernel(x, positions, pre_norm_w, w_in, merge_bias, lambda_q1, lambda_k1, lambda_q2,
           lambda_k2, subln_w, w_att_out, conv_w, w_conv_out, w_out, post_norm_w):
    batch, seq, _ = x.shape
    depth = w_in.shape[0]
    x2 = x.reshape(batch * seq, D_MODEL)
    for layer in range(depth):
        lambda_init = 0.8 - 0.6 * math.exp(-0.3 * layer)
        x2 = _layer(x2, positions, pre_norm_w[layer], w_in[layer], merge_bias[layer],
                    lambda_q1[layer], lambda_k1[layer], lambda_q2[layer], lambda_k2[layer],
                    subln_w[layer], w_att_out[layer], conv_w[layer], w_conv_out[layer],
                    w_out[layer], post_norm_w[layer], lambda_init, batch, seq)
    return x2.reshape(batch, seq, D_MODEL)
```

```python
import functools
import math

import jax
import jax.numpy as jnp
from jax import lax
from jax.experimental import pallas as pl
from jax.experimental.pallas import tpu as pltpu

D_MODEL = 1024
CHUNK_SHIFT = 6
HEADS = 4
HEAD_DIM = 64
V_DIM = 2 * HEAD_DIM
LANES = 128
BF16_ROWS = 16
VX_DIM = V_DIM + BF16_ROWS
QK_WIDTH = 2 * HEADS * HEAD_DIM
ATT_WIDTH = HEADS * V_DIM
CONV_WIDTH = D_MODEL // 2
CONV_K = 3
ROPE_THETA = 500000.0
ROPE_DIM = HEAD_DIM // 4
ROPE_HALF = ROPE_DIM // 2
NORM_EPS = 1e-6
SUBLN_EPS = 1e-5
REST_WIDTH = ATT_WIDTH + 4 * CONV_WIDTH + 2 * D_MODEL
QKV_WIDTH = 2 * QK_WIDTH + ATT_WIDTH
WPREP_COLS = 512

TM = 256
TMP = 1024
TQ = 512
TK = 512
FREE_TILE_GROUPS = (2, 1)
TMO = 1024
OUT_ROWS = 256
OUT_RING = 3

VMEM_LIMIT = 56 * 1024 * 1024
NEG = float(jnp.finfo(jnp.float32).min) / 2

_NT = (((1,), (1,)), ((), ()))


def _resident(block_shape, index_map):
    return pl.BlockSpec(block_shape, index_map, pipeline_mode=pl.Buffered(1))


def _wprep_kernel(w_ref, wT_ref, wrest_ref, *, n_transposed):
    c = pl.program_id(0)

    @pl.when(c < n_transposed)
    def _():
        wT_ref[...] = w_ref[...].T.astype(jnp.bfloat16)

    @pl.when(c >= n_transposed)
    def _():
        wrest_ref[...] = w_ref[...].astype(jnp.bfloat16)


def _proj_kernel(x_ref, pos_ref, freq_ref, prew_ref, wT_ref, wrest_ref, mb_ref, cw_ref,
                 qT_ref, k_ref, vT_ref, sz_ref, cvs_ref, gate_ref, cu_prev_ref):
    f32 = jnp.float32
    cw = CONV_WIDTH

    @pl.when(pl.program_id(1) == 0)
    def _():
        cu_prev_ref[...] = jnp.zeros(cu_prev_ref.shape, f32)

    def sigmoid(v):
        return 0.5 * jnp.tanh(0.5 * v) + 0.5

    def chain(ci, prev):
        rows = slice(ci * TM, (ci + 1) * TM)
        x = x_ref[rows, :]
        ms = jnp.mean(x * x, axis=-1, keepdims=True)
        h = (x * lax.rsqrt(ms + NORM_EPS)) * prew_ref[...]
        hb = h.astype(jnp.bfloat16)

        def cols(c0, width):
            return jnp.dot(hb, wrest_ref[:, c0:c0 + width], preferred_element_type=f32)

        c0 = ATT_WIDTH + 4 * cw
        for c in range(0, 2 * D_MODEL, 512):
            bias = mb_ref[c // D_MODEL:c // D_MODEL + 1, c % D_MODEL:c % D_MODEL + 512]
            gate_ref[rows, c:c + 512] = sigmoid(cols(c0 + c, 512) + bias).astype(jnp.bfloat16)

        z_att = cols(0, ATT_WIDTH)
        sz_ref[rows, :] = (z_att * sigmoid(z_att)).astype(jnp.bfloat16)

        c0 = ATT_WIDTH
        gb = cols(c0, cw)
        cu = cols(c0 + cw, cw) * cols(c0 + 2 * cw, cw)
        z_conv = cols(c0 + 3 * cw, cw)

        h1 = prev[7:8]
        h2 = prev[6:7]
        row = lax.broadcasted_iota(jnp.int32, (8, cw), 0)

        def shifted(by, fix_top):
            r = pltpu.roll(cu, by, 0)
            return jnp.concatenate([fix_top(r[0:8]), r[8:]], axis=0)

        r1 = shifted(1, lambda top: jnp.where(row == 0, h1, top))
        r2 = shifted(2, lambda top: jnp.where(row == 0, h2, jnp.where(row == 1, h1, top)))
        conv = cw_ref[0:1, :] * r2 + cw_ref[1:2, :] * r1 + cw_ref[2:3, :] * cu
        cvs_ref[rows, :] = ((gb * conv) * (z_conv * sigmoid(z_conv))).astype(jnp.bfloat16)

        qkT = lax.dot_general(wT_ref[0:2 * QK_WIDTH, :], hb, _NT, preferred_element_type=f32)

        pos = pos_ref[0, :, rows].astype(jnp.float32)
        ang = freq_ref[...] * pos
        cos = jnp.cos(ang)
        sin = jnp.sin(ang)

        def rot(blk):
            t1 = blk[0:ROPE_HALF]
            t2 = blk[ROPE_HALF:ROPE_DIM]
            return jnp.concatenate(
                [t1 * cos - t2 * sin, t2 * cos + t1 * sin, blk[ROPE_DIM:]], axis=0)

        scale = math.log2(math.e) / math.sqrt(HEAD_DIM)
        for hd in range(2 * HEADS):
            r0 = hd * HEAD_DIM
            qT_ref[0, r0:r0 + HEAD_DIM, rows] = (
                rot(qkT[r0:r0 + HEAD_DIM]) * scale).astype(jnp.bfloat16)
        kT = jnp.concatenate(
            [rot(qkT[QK_WIDTH + hd * HEAD_DIM:QK_WIDTH + (hd + 1) * HEAD_DIM])
             for hd in range(2 * HEADS)], axis=0)
        k = kT.T.astype(jnp.bfloat16)
        for p in range(HEADS):
            k_ref[0, p, rows, :] = k[:, p * V_DIM:(p + 1) * V_DIM]

        vT = lax.dot_general(wT_ref[2 * QK_WIDTH:QKV_WIDTH, :], hb, _NT,
                             preferred_element_type=f32)
        one_row = lax.broadcasted_iota(jnp.int32, (BF16_ROWS, TM), 0) == 0
        ones_rows = jnp.where(one_row, 1.0, 0.0).astype(jnp.bfloat16)
        for p in range(HEADS):
            tile, lanes = divmod(ci * TM, TK)
            vT_ref[0, tile, p * VX_DIM:p * VX_DIM + V_DIM, lanes:lanes + TM] = (
                vT[p * V_DIM:(p + 1) * V_DIM].astype(jnp.bfloat16))
            vT_ref[0, tile, p * VX_DIM + V_DIM:(p + 1) * VX_DIM, lanes:lanes + TM] = ones_rows
        return cu[TM - 8:TM]

    prev = cu_prev_ref[...]
    for ci in range(TMP // TM):
        prev = chain(ci, prev)
    cu_prev_ref[...] = prev


def _attn_kernel(klist_ref, nunm_ref, ntot_ref,
                 qT_ref, k_ref, vT_ref, posr_ref, posk_ref,
                 lq1_ref, lk1_ref, lq2_ref, lk2_ref, sw_ref, sz_ref,
                 o_ref, q2_ref, acc_ref, m_ref, sa_ref, sb_ref, mt0_ref,
                 *, nq, nk, lambda_init):
    qt = pl.program_id(0) * nq + pl.program_id(1)
    n = ntot_ref[qt]
    n_unm = nunm_ref[qt]

    qc = lax.shift_right_arithmetic(posr_ref[0], CHUNK_SHIFT)
    qc2 = jnp.concatenate([qc, qc], axis=1)

    def init_state():
        for p in range(HEADS):
            q2_ref[p] = jnp.zeros(q2_ref.shape[1:], q2_ref.dtype)
            q2_ref[p, 0:HEAD_DIM, 0:TQ] = qT_ref[0, p * V_DIM:p * V_DIM + HEAD_DIM, :]
            q2_ref[p, HEAD_DIM:V_DIM, TQ:2 * TQ] = (
                qT_ref[0, p * V_DIM + HEAD_DIM:(p + 1) * V_DIM, :])
            m_ref[p] = jnp.full(m_ref.shape[1:], NEG, jnp.float32)
            acc_ref[p] = jnp.zeros(acc_ref.shape[1:], jnp.float32)

    def key_tile(r):
        return klist_ref[qt * nk + r]

    def score(p, j, s_ref, masked):
        k0 = pl.multiple_of(j * TK, TK)
        s = jnp.dot(k_ref[0, p, pl.ds(k0, TK), :], q2_ref[p],
                    preferred_element_type=jnp.float32)
        if masked:
            kc = lax.shift_right_arithmetic(posk_ref[0, j], CHUNK_SHIFT)
            kc = jnp.broadcast_to(kc, (LANES, TK)).T
            kc = jnp.concatenate([kc] * (2 * TQ // LANES), axis=1)
            s = jnp.where(kc <= qc2, s, NEG)
        s_ref[...] = s
        return jnp.max(s, axis=0, keepdims=True)

    def update(p, j, s_ref, mt):
        m_old = m_ref[p]
        m_new = jnp.maximum(m_old, mt)
        alpha = jnp.exp2(m_old - m_new)
        pr = jnp.exp2(s_ref[...] - m_new).astype(jnp.bfloat16)
        vt = vT_ref[0, j, p * VX_DIM:(p + 1) * VX_DIM, :]
        acc_ref[p] = alpha * acc_ref[p] + jnp.dot(vt, pr, preferred_element_type=jnp.float32)
        m_ref[p] = m_new

    def block(j, j_next, masked, next_masked):
        mt1 = score(1, j, sb_ref, masked)
        update(0, j, sa_ref, mt0_ref[...])
        mt2 = score(2, j, sa_ref, masked)
        update(1, j, sb_ref, mt1)
        mt3 = score(3, j, sb_ref, masked)
        update(2, j, sa_ref, mt2)
        if next_masked is not None:
            mt0_ref[...] = score(0, j_next, sa_ref, next_masked)
        update(3, j, sb_ref, mt3)

    @pl.when(n_unm > 0)
    def _():
        init_state()
        mt0_ref[...] = score(0, key_tile(0), sa_ref, False)

    @pl.when(n_unm == 0)
    def _():
        init_state()
        mt0_ref[...] = score(0, key_tile(0), sa_ref, True)

    done = 0
    for group in FREE_TILE_GROUPS:
        n_groups = jnp.maximum(n_unm - 1 - done, 0) // group

        def free_blocks(g, carry, group=group, done=done):
            r0 = done + g * group
            for r in range(group):
                block(key_tile(r0 + r), key_tile(r0 + r + 1), False, False)
            return carry

        lax.fori_loop(0, n_groups, free_blocks, 0)
        done = done + n_groups * group

    @pl.when((n_unm > 0) & (n > n_unm))
    def _():
        block(key_tile(n_unm - 1), key_tile(n_unm), False, True)

    @pl.when((n_unm > 0) & (n == n_unm))
    def _():
        block(key_tile(n_unm - 1), None, False, None)

    def masked_block(r, carry):
        block(key_tile(r), key_tile(r + 1), True, True)
        return carry

    lax.fori_loop(n_unm, n - 1, masked_block, 0)

    @pl.when(n > n_unm)
    def _():
        block(key_tile(n - 1), None, True, None)

    lam = (jnp.exp(jnp.sum(lq1_ref[...] * lk1_ref[...], axis=-1, keepdims=True))
           - jnp.exp(jnp.sum(lq2_ref[...] * lk2_ref[...], axis=-1, keepdims=True))
           + lambda_init)
    for p in range(HEADS):
        o = acc_ref[p, 0:V_DIM, :] / acc_ref[p, V_DIM:V_DIM + 1, :]
        d = o[:, :TQ] - lam * o[:, TQ:]
        ms = jnp.mean(d * d, axis=0, keepdims=True)
        dn = ((d * lax.rsqrt(ms + SUBLN_EPS)) * sw_ref[...]) * (1.0 - lambda_init)
        lanes = slice(p * V_DIM, (p + 1) * V_DIM)
        o_ref[:, lanes] = (dn.T * sz_ref[:, lanes].astype(jnp.float32)).astype(o_ref.dtype)


def _out_kernel(x_hbm, att_ref, cvs_ref, gate_hbm, wao_ref, wco_ref, wo_ref, postw_ref, o_ref,
                xbuf, gbuf, xsem, gsem, *, n_steps):
    f32 = jnp.float32
    t = pl.program_id(0)

    def copies(step, slot):
        rows = pl.ds(pl.multiple_of(step * TMO, TMO), TMO)
        return (pltpu.make_async_copy(x_hbm.at[rows, :], xbuf.at[slot], xsem.at[slot]),
                pltpu.make_async_copy(gate_hbm.at[rows, :], gbuf.at[slot], gsem.at[slot]))

    @pl.when(t == 0)
    def _():
        for s in range(OUT_RING - 1):
            for cp in copies(s, s):
                cp.start()

    ahead = t + (OUT_RING - 1)

    @pl.when(ahead < n_steps)
    def _():
        for cp in copies(ahead, ahead % OUT_RING):
            cp.start()

    slot = t % OUT_RING
    for cp in copies(t, slot):
        cp.wait()
    x_ref = xbuf.at[slot]
    gate_ref = gbuf.at[slot]

    for r0 in range(0, TMO, OUT_ROWS):
        rows = slice(r0, r0 + OUT_ROWS)
        y_att = jnp.dot(att_ref[rows, :], wao_ref[...], preferred_element_type=f32)
        y_conv = jnp.dot(cvs_ref[rows, :], wco_ref[...], preferred_element_type=f32)
        m = (gate_ref[rows, 0:D_MODEL].astype(f32) * y_att
             + gate_ref[rows, D_MODEL:2 * D_MODEL].astype(f32) * y_conv)
        o = jnp.dot(m.astype(jnp.bfloat16), wo_ref[...], preferred_element_type=f32)
        ms = jnp.mean(o * o, axis=-1, keepdims=True)
        o_ref[rows, :] = x_ref[rows, :] + (o * lax.rsqrt(ms + NORM_EPS)) * postw_ref[...]


def _key_tile_lists(positions, batch, nq, nk):
    chunk = lax.shift_right_arithmetic(positions, CHUNK_SHIFT)
    kch = chunk.reshape(batch, 1, nk, TK)
    qch = chunk.reshape(batch, nq, 1, TQ)
    needed = kch.min(-1) <= qch.max(-1)
    nomask = needed & (kch.max(-1) <= qch.min(-1))
    rank = jnp.where(nomask, 0, jnp.where(needed, 1, 2))
    order = jnp.argsort(rank, axis=-1, stable=True).astype(jnp.int32)

    def count(flag):
        return flag.sum(-1).astype(jnp.int32).reshape(-1)

    return order.reshape(-1), count(nomask), count(needed)


def _layer(x2, positions, pre_w, w_in, merge_bias, lq1, lk1, lq2, lk2, subln_w,
           w_att_out, conv_w, w_conv_out, w_out, post_w, lambda_init, batch, seq):
    bf16 = jnp.bfloat16
    n = batch * seq
    nt = seq // TMP
    nq = seq // TQ
    nk = seq // TK
    assert seq % TMP == 0 and seq % TQ == 0 and TK % TM == 0 and n % TMO == 0

    freqs = (ROPE_THETA ** (-jnp.arange(0, ROPE_DIM, 2, dtype=jnp.float32) / ROPE_DIM)
             ).reshape(ROPE_HALF, 1)
    n_t = QKV_WIDTH // WPREP_COLS
    wT, wrest = pl.pallas_call(
        functools.partial(_wprep_kernel, n_transposed=n_t),
        grid=((QKV_WIDTH + REST_WIDTH) // WPREP_COLS,),
        in_specs=[pl.BlockSpec((D_MODEL, WPREP_COLS), lambda c: (0, c))],
        out_specs=[
            pl.BlockSpec((WPREP_COLS, D_MODEL), lambda c: (jnp.minimum(c, n_t - 1), 0)),
            pl.BlockSpec((D_MODEL, WPREP_COLS), lambda c: (0, jnp.maximum(c - n_t, 0))),
        ],
        out_shape=[
            jax.ShapeDtypeStruct((QKV_WIDTH, D_MODEL), bf16),
            jax.ShapeDtypeStruct((D_MODEL, REST_WIDTH), bf16),
        ],
        compiler_params=pltpu.CompilerParams(dimension_semantics=("arbitrary",)),
        name="wprep",
    )(w_in)
    pos_row = positions.reshape(batch, 1, seq)
    pos_tiles = positions.reshape(batch, nk, 1, TK)

    cparams = functools.partial(pltpu.CompilerParams, vmem_limit_bytes=VMEM_LIMIT)

    def rows(width):
        return pl.BlockSpec((TMP, width), lambda b, t: (b * nt + t, 0))

    qT, k, vT, sz, cvs, gates = pl.pallas_call(
        _proj_kernel,
        grid=(batch, nt),
        in_specs=[
            rows(D_MODEL),
            pl.BlockSpec((1, 1, TMP), lambda b, t: (b, 0, t)),
            _resident((ROPE_HALF, 1), lambda b, t: (0, 0)),
            _resident((1, D_MODEL), lambda b, t: (0, 0)),
            _resident((QKV_WIDTH, D_MODEL), lambda b, t: (0, 0)),
            _resident((D_MODEL, REST_WIDTH), lambda b, t: (0, 0)),
            _resident((2, D_MODEL), lambda b, t: (0, 0)),
            _resident((CONV_K, CONV_WIDTH), lambda b, t: (0, 0)),
        ],
        out_specs=[
            pl.BlockSpec((1, QK_WIDTH, TMP), lambda b, t: (b, 0, t)),
            pl.BlockSpec((1, HEADS, TMP, V_DIM), lambda b, t: (b, 0, t, 0)),
            pl.BlockSpec((1, TMP // TK, HEADS * VX_DIM, TK), lambda b, t: (b, t, 0, 0)),
            rows(ATT_WIDTH),
            rows(CONV_WIDTH),
            rows(2 * D_MODEL),
        ],
        out_shape=[
            jax.ShapeDtypeStruct((batch, QK_WIDTH, seq), bf16),
            jax.ShapeDtypeStruct((batch, HEADS, seq, V_DIM), bf16),
            jax.ShapeDtypeStruct((batch, nk, HEADS * VX_DIM, TK), bf16),
            jax.ShapeDtypeStruct((n, ATT_WIDTH), bf16),
            jax.ShapeDtypeStruct((n, CONV_WIDTH), bf16),
            jax.ShapeDtypeStruct((n, 2 * D_MODEL), bf16),
        ],
        scratch_shapes=[pltpu.VMEM((8, CONV_WIDTH), jnp.float32)],
        compiler_params=cparams(dimension_semantics=("arbitrary", "arbitrary")),
        name="proj",
    )(x2, pos_row, freqs, pre_w.reshape(1, D_MODEL), wT, wrest,
      merge_bias, conv_w.reshape(CONV_K, CONV_WIDTH))

    klist, n_unm, n_tot = _key_tile_lists(positions, batch, nq, nk)

    att = pl.pallas_call(
        functools.partial(_attn_kernel, nq=nq, nk=nk, lambda_init=lambda_init),
        grid_spec=pltpu.PrefetchScalarGridSpec(
            num_scalar_prefetch=3,
            grid=(batch, nq),
            in_specs=[
                pl.BlockSpec((1, QK_WIDTH, TQ), lambda b, i, *_: (b, 0, i)),
                pl.BlockSpec((1, HEADS, seq, V_DIM), lambda b, i, *_: (b, 0, 0, 0)),
                pl.BlockSpec((1, nk, HEADS * VX_DIM, TK), lambda b, i, *_: (b, 0, 0, 0)),
                pl.BlockSpec((1, 1, TQ), lambda b, i, *_: (b, 0, i)),
                pl.BlockSpec((1, nk, 1, TK), lambda b, i, *_: (b, 0, 0, 0)),
                _resident((1, HEAD_DIM), lambda b, i, *_: (0, 0)),
                _resident((1, HEAD_DIM), lambda b, i, *_: (0, 0)),
                _resident((1, HEAD_DIM), lambda b, i, *_: (0, 0)),
                _resident((1, HEAD_DIM), lambda b, i, *_: (0, 0)),
                _resident((V_DIM, 1), lambda b, i, *_: (0, 0)),
                pl.BlockSpec((TQ, ATT_WIDTH), lambda b, i, *_: (b * nq + i, 0)),
            ],
            out_specs=pl.BlockSpec((TQ, ATT_WIDTH), lambda b, i, *_: (b * nq + i, 0)),
            scratch_shapes=[
                pltpu.VMEM((HEADS, V_DIM, 2 * TQ), bf16),
                pltpu.VMEM((HEADS, VX_DIM, 2 * TQ), jnp.float32),
                pltpu.VMEM((HEADS, 1, 2 * TQ), jnp.float32),
                pltpu.VMEM((TK, 2 * TQ), jnp.float32),
                pltpu.VMEM((TK, 2 * TQ), jnp.float32),
                pltpu.VMEM((1, 2 * TQ), jnp.float32),
            ],
        ),
        out_shape=jax.ShapeDtypeStruct((n, ATT_WIDTH), bf16),
        compiler_params=cparams(dimension_semantics=("arbitrary", "arbitrary")),
        name="attn",
    )(klist, n_unm, n_tot, qT, k, vT, pos_row, pos_tiles,
      lq1.reshape(1, HEAD_DIM), lk1.reshape(1, HEAD_DIM),
      lq2.reshape(1, HEAD_DIM), lk2.reshape(1, HEAD_DIM), subln_w.reshape(V_DIM, 1), sz)

    out_steps = n // TMO
    assert out_steps >= OUT_RING - 1
    out = pl.pallas_call(
        functools.partial(_out_kernel, n_steps=out_steps),
        grid=(out_steps,),
        in_specs=[
            pl.BlockSpec(memory_space=pl.ANY),
            pl.BlockSpec((TMO, ATT_WIDTH), lambda t: (t, 0)),
            pl.BlockSpec((TMO, CONV_WIDTH), lambda t: (t, 0)),
            pl.BlockSpec(memory_space=pl.ANY),
            _resident((ATT_WIDTH, D_MODEL), lambda t: (0, 0)),
            _resident((CONV_WIDTH, D_MODEL), lambda t: (0, 0)),
            _resident((D_MODEL, D_MODEL), lambda t: (0, 0)),
            _resident((1, D_MODEL), lambda t: (0, 0)),
        ],
        out_specs=pl.BlockSpec((TMO, D_MODEL), lambda t: (t, 0)),
        out_shape=jax.ShapeDtypeStruct((n, D_MODEL), jnp.float32),
        scratch_shapes=[
            pltpu.VMEM((OUT_RING, TMO, D_MODEL), jnp.float32),
            pltpu.VMEM((OUT_RING, TMO, 2 * D_MODEL), bf16),
            pltpu.SemaphoreType.DMA((OUT_RING,)),
            pltpu.SemaphoreType.DMA((OUT_RING,)),
        ],
        compiler_params=cparams(dimension_semantics=("arbitrary",)),
        name="outp",
    )(x2, att, cvs, gates, w_att_out.astype(bf16), w_conv_out.astype(bf16), w_out.astype(bf16),
      post_w.reshape(1, D_MODEL))
    return out


def kernel(x, positions, pre_norm_w, w_in, merge_bias, lambda_q1, lambda_k1, lambda_q2,
           lambda_k2, subln_w, w_att_out, conv_w, w_conv_out, w_out, post_norm_w):
    batch, seq, _ = x.shape
    depth = w_in.shape[0]
    x2 = x.reshape(batch * seq, D_MODEL)
    for layer in range(depth):
        lambda_init = 0.8 - 0.6 * math.exp(-0.3 * layer)
        x2 = _layer(x2, positions, pre_norm_w[layer], w_in[layer], merge_bias[layer],
                    lambda_q1[layer], lambda_k1[layer], lambda_q2[layer], lambda_k2[layer],
                    subln_w[layer], w_att_out[layer], conv_w[layer], w_conv_out[layer],
                    w_out[layer], post_norm_w[layer], lambda_init, batch, seq)
    return x2.reshape(batch, seq, D_MODEL)
```

```python
import functools
import math

import jax
import jax.numpy as jnp
from jax import lax
from jax.experimental import pallas as pl
from jax.experimental.pallas import tpu as pltpu

D_MODEL = 1024
CHUNK_SHIFT = 6
HEADS = 4
HEAD_DIM = 64
V_DIM = 2 * HEAD_DIM
LANES = 128
BF16_ROWS = 16
VX_DIM = V_DIM + BF16_ROWS
QK_WIDTH = 2 * HEADS * HEAD_DIM
ATT_WIDTH = HEADS * V_DIM
CONV_WIDTH = D_MODEL // 2
CONV_K = 3
ROPE_THETA = 500000.0
ROPE_DIM = HEAD_DIM // 4
ROPE_HALF = ROPE_DIM // 2
NORM_EPS = 1e-6
SUBLN_EPS = 1e-5
REST_WIDTH = ATT_WIDTH + 4 * CONV_WIDTH + 2 * D_MODEL
QKV_WIDTH = 2 * QK_WIDTH + ATT_WIDTH
WPREP_COLS = 512

TM = 256
TMP = 1024
TQ = 512
TK = 512
FREE_TILE_GROUPS = (2, 1)
TMO = 1024
OUT_ROWS = 256

VMEM_LIMIT = 56 * 1024 * 1024
NEG = float(jnp.finfo(jnp.float32).min) / 2

_NT = (((1,), (1,)), ((), ()))


def _resident(block_shape, index_map):
    return pl.BlockSpec(block_shape, index_map, pipeline_mode=pl.Buffered(1))


def _wprep_kernel(w_ref, wT_ref, wrest_ref, *, n_transposed):
    c = pl.program_id(0)

    @pl.when(c < n_transposed)
    def _():
        wT_ref[...] = w_ref[...].T.astype(jnp.bfloat16)

    @pl.when(c >= n_transposed)
    def _():
        wrest_ref[...] = w_ref[...].astype(jnp.bfloat16)


def _proj_kernel(x_ref, pos_ref, freq_ref, prew_ref, wT_ref, wrest_ref, mb_ref, cw_ref,
                 qT_ref, k_ref, vT_ref, sz_ref, cvs_ref, gate_ref, cu_prev_ref):
    f32 = jnp.float32
    cw = CONV_WIDTH

    @pl.when(pl.program_id(1) == 0)
    def _():
        cu_prev_ref[...] = jnp.zeros(cu_prev_ref.shape, f32)

    def sigmoid(v):
        return 0.5 * jnp.tanh(0.5 * v) + 0.5

    def chain(ci, prev):
        rows = slice(ci * TM, (ci + 1) * TM)
        x = x_ref[rows, :]
        ms = jnp.mean(x * x, axis=-1, keepdims=True)
        h = (x * lax.rsqrt(ms + NORM_EPS)) * prew_ref[...]
        hb = h.astype(jnp.bfloat16)

        def cols(c0, width):
            return jnp.dot(hb, wrest_ref[:, c0:c0 + width], preferred_element_type=f32)

        c0 = ATT_WIDTH + 4 * cw
        for c in range(0, 2 * D_MODEL, 512):
            bias = mb_ref[c // D_MODEL:c // D_MODEL + 1, c % D_MODEL:c % D_MODEL + 512]
            gate_ref[rows, c:c + 512] = sigmoid(cols(c0 + c, 512) + bias).astype(jnp.bfloat16)

        z_att = cols(0, ATT_WIDTH)
        sz_ref[rows, :] = (z_att * sigmoid(z_att)).astype(jnp.bfloat16)

        c0 = ATT_WIDTH
        gb = cols(c0, cw)
        cu = cols(c0 + cw, cw) * cols(c0 + 2 * cw, cw)
        z_conv = cols(c0 + 3 * cw, cw)

        h1 = prev[7:8]
        h2 = prev[6:7]
        row = lax.broadcasted_iota(jnp.int32, (8, cw), 0)

        def shifted(by, fix_top):
            r = pltpu.roll(cu, by, 0)
            return jnp.concatenate([fix_top(r[0:8]), r[8:]], axis=0)

        r1 = shifted(1, lambda top: jnp.where(row == 0, h1, top))
        r2 = shifted(2, lambda top: jnp.where(row == 0, h2, jnp.where(row == 1, h1, top)))
        conv = cw_ref[0:1, :] * r2 + cw_ref[1:2, :] * r1 + cw_ref[2:3, :] * cu
        cvs_ref[rows, :] = ((gb * conv) * (z_conv * sigmoid(z_conv))).astype(jnp.bfloat16)

        qkT = lax.dot_general(wT_ref[0:2 * QK_WIDTH, :], hb, _NT, preferred_element_type=f32)

        pos = pos_ref[0, :, rows].astype(jnp.float32)
        ang = freq_ref[...] * pos
        cos = jnp.cos(ang)
        sin = jnp.sin(ang)

        def rot(blk):
            t1 = blk[0:ROPE_HALF]
            t2 = blk[ROPE_HALF:ROPE_DIM]
            return jnp.concatenate(
                [t1 * cos - t2 * sin, t2 * cos + t1 * sin, blk[ROPE_DIM:]], axis=0)

        scale = math.log2(math.e) / math.sqrt(HEAD_DIM)
        for hd in range(2 * HEADS):
            r0 = hd * HEAD_DIM
            qT_ref[0, r0:r0 + HEAD_DIM, rows] = (
                rot(qkT[r0:r0 + HEAD_DIM]) * scale).astype(jnp.bfloat16)
        kT = jnp.concatenate(
            [rot(qkT[QK_WIDTH + hd * HEAD_DIM:QK_WIDTH + (hd + 1) * HEAD_DIM])
             for hd in range(2 * HEADS)], axis=0)
        k = kT.T.astype(jnp.bfloat16)
        for p in range(HEADS):
            k_ref[0, p, rows, :] = k[:, p * V_DIM:(p + 1) * V_DIM]

        vT = lax.dot_general(wT_ref[2 * QK_WIDTH:QKV_WIDTH, :], hb, _NT,
                             preferred_element_type=f32)
        one_row = lax.broadcasted_iota(jnp.int32, (BF16_ROWS, TM), 0) == 0
        ones_rows = jnp.where(one_row, 1.0, 0.0).astype(jnp.bfloat16)
        for p in range(HEADS):
            tile, lanes = divmod(ci * TM, TK)
            vT_ref[0, tile, p * VX_DIM:p * VX_DIM + V_DIM, lanes:lanes + TM] = (
                vT[p * V_DIM:(p + 1) * V_DIM].astype(jnp.bfloat16))
            vT_ref[0, tile, p * VX_DIM + V_DIM:(p + 1) * VX_DIM, lanes:lanes + TM] = ones_rows
        return cu[TM - 8:TM]

    prev = cu_prev_ref[...]
    for ci in range(TMP // TM):
        prev = chain(ci, prev)
    cu_prev_ref[...] = prev


def _attn_kernel(klist_ref, nunm_ref, ntot_ref,
                 qT_ref, k_ref, vT_ref, posr_ref, posk_ref,
                 lq1_ref, lk1_ref, lq2_ref, lk2_ref, sw_ref, sz_ref,
                 o_ref, q2_ref, acc_ref, m_ref, sa_ref, sb_ref, mt0_ref,
                 *, nq, nk, lambda_init):
    qt = pl.program_id(0) * nq + pl.program_id(1)
    n = ntot_ref[qt]
    n_unm = nunm_ref[qt]

    qc = lax.shift_right_arithmetic(posr_ref[0], CHUNK_SHIFT)
    qc2 = jnp.concatenate([qc, qc], axis=1)

    def init_state():
        for p in range(HEADS):
            q2_ref[p] = jnp.zeros(q2_ref.shape[1:], q2_ref.dtype)
            q2_ref[p, 0:HEAD_DIM, 0:TQ] = qT_ref[0, p * V_DIM:p * V_DIM + HEAD_DIM, :]
            q2_ref[p, HEAD_DIM:V_DIM, TQ:2 * TQ] = (
                qT_ref[0, p * V_DIM + HEAD_DIM:(p + 1) * V_DIM, :])
            m_ref[p] = jnp.full(m_ref.shape[1:], NEG, jnp.float32)
            acc_ref[p] = jnp.zeros(acc_ref.shape[1:], jnp.float32)

    def key_tile(r):
        return klist_ref[qt * nk + r]

    def score(p, j, s_ref, masked):
        k0 = pl.multiple_of(j * TK, TK)
        s = jnp.dot(k_ref[0, p, pl.ds(k0, TK), :], q2_ref[p],
                    preferred_element_type=jnp.float32)
        if masked:
            kc = lax.shift_right_arithmetic(posk_ref[0, j], CHUNK_SHIFT)
            kc = jnp.broadcast_to(kc, (LANES, TK)).T
            kc = jnp.concatenate([kc] * (2 * TQ // LANES), axis=1)
            s = jnp.where(kc <= qc2, s, NEG)
        s_ref[...] = s
        return jnp.max(s, axis=0, keepdims=True)

    def update(p, j, s_ref, mt):
        m_old = m_ref[p]
        m_new = jnp.maximum(m_old, mt)
        alpha = jnp.exp2(m_old - m_new)
        pr = jnp.exp2(s_ref[...] - m_new).astype(jnp.bfloat16)
        vt = vT_ref[0, j, p * VX_DIM:(p + 1) * VX_DIM, :]
        acc_ref[p] = alpha * acc_ref[p] + jnp.dot(vt, pr, preferred_element_type=jnp.float32)
        m_ref[p] = m_new

    def block(j, j_next, masked, next_masked):
        mt1 = score(1, j, sb_ref, masked)
        update(0, j, sa_ref, mt0_ref[...])
        mt2 = score(2, j, sa_ref, masked)
        update(1, j, sb_ref, mt1)
        mt3 = score(3, j, sb_ref, masked)
        update(2, j, sa_ref, mt2)
        if next_masked is not None:
            mt0_ref[...] = score(0, j_next, sa_ref, next_masked)
        update(3, j, sb_ref, mt3)

    @pl.when(n_unm > 0)
    def _():
        init_state()
        mt0_ref[...] = score(0, key_tile(0), sa_ref, False)

    @pl.when(n_unm == 0)
    def _():
        init_state()
        mt0_ref[...] = score(0, key_tile(0), sa_ref, True)

    done = 0
    for group in FREE_TILE_GROUPS:
        n_groups = jnp.maximum(n_unm - 1 - done, 0) // group

        def free_blocks(g, carry, group=group, done=done):
            r0 = done + g * group
            for r in range(group):
                block(key_tile(r0 + r), key_tile(r0 + r + 1), False, False)
            return carry

        lax.fori_loop(0, n_groups, free_blocks, 0)
        done = done + n_groups * group

    @pl.when((n_unm > 0) & (n > n_unm))
    def _():
        block(key_tile(n_unm - 1), key_tile(n_unm), False, True)

    @pl.when((n_unm > 0) & (n == n_unm))
    def _():
        block(key_tile(n_unm - 1), None, False, None)

    def masked_block(r, carry):
        block(key_tile(r), key_tile(r + 1), True, True)
        return carry

    lax.fori_loop(n_unm, n - 1, masked_block, 0)

    @pl.when(n > n_unm)
    def _():
        block(key_tile(n - 1), None, True, None)

    lam = (jnp.exp(jnp.sum(lq1_ref[...] * lk1_ref[...], axis=-1, keepdims=True))
           - jnp.exp(jnp.sum(lq2_ref[...] * lk2_ref[...], axis=-1, keepdims=True))
           + lambda_init)
    for p in range(HEADS):
        o = acc_ref[p, 0:V_DIM, :] / acc_ref[p, V_DIM:V_DIM + 1, :]
        d = o[:, :TQ] - lam * o[:, TQ:]
        ms = jnp.mean(d * d, axis=0, keepdims=True)
        dn = ((d * lax.rsqrt(ms + SUBLN_EPS)) * sw_ref[...]) * (1.0 - lambda_init)
        lanes = slice(p * V_DIM, (p + 1) * V_DIM)
        o_ref[:, lanes] = (dn.T * sz_ref[:, lanes].astype(jnp.float32)).astype(o_ref.dtype)


def _out_kernel(x_ref, att_ref, cvs_ref, gate_ref, wao_ref, wco_ref, wo_ref, postw_ref, o_ref,
                wao_b, wco_b, wo_b):
    f32 = jnp.float32

    @pl.when(pl.program_id(0) == 0)
    def _():
        wao_b[...] = wao_ref[...].astype(jnp.bfloat16)
        wco_b[...] = wco_ref[...].astype(jnp.bfloat16)
        wo_b[...] = wo_ref[...].astype(jnp.bfloat16)

    for r0 in range(0, TMO, OUT_ROWS):
        rows = slice(r0, r0 + OUT_ROWS)
        y_att = jnp.dot(att_ref[rows, :], wao_b[...], preferred_element_type=f32)
        y_conv = jnp.dot(cvs_ref[rows, :], wco_b[...], preferred_element_type=f32)
        m = (gate_ref[rows, 0:D_MODEL].astype(f32) * y_att
             + gate_ref[rows, D_MODEL:2 * D_MODEL].astype(f32) * y_conv)
        o = jnp.dot(m.astype(jnp.bfloat16), wo_b[...], preferred_element_type=f32)
        ms = jnp.mean(o * o, axis=-1, keepdims=True)
        o_ref[rows, :] = x_ref[rows, :] + (o * lax.rsqrt(ms + NORM_EPS)) * postw_ref[...]


def _key_tile_lists(positions, batch, nq, nk):
    chunk = lax.shift_right_arithmetic(positions, CHUNK_SHIFT)
    kch = chunk.reshape(batch, 1, nk, TK)
    qch = chunk.reshape(batch, nq, 1, TQ)
    needed = kch.min(-1) <= qch.max(-1)
    nomask = needed & (kch.max(-1) <= qch.min(-1))
    rank = jnp.where(nomask, 0, jnp.where(needed, 1, 2))
    order = jnp.argsort(rank, axis=-1, stable=True).astype(jnp.int32)

    def count(flag):
        return flag.sum(-1).astype(jnp.int32).reshape(-1)

    return order.reshape(-1), count(nomask), count(needed)


def _layer(x2, positions, pre_w, w_in, merge_bias, lq1, lk1, lq2, lk2, subln_w,
           w_att_out, conv_w, w_conv_out, w_out, post_w, lambda_init, batch, seq):
    bf16 = jnp.bfloat16
    n = batch * seq
    nt = seq // TMP
    nq = seq // TQ
    nk = seq // TK
    assert seq % TMP == 0 and seq % TQ == 0 and TK % TM == 0 and n % TMO == 0

    freqs = (ROPE_THETA ** (-jnp.arange(0, ROPE_DIM, 2, dtype=jnp.float32) / ROPE_DIM)
             ).reshape(ROPE_HALF, 1)
    n_t = QKV_WIDTH // WPREP_COLS
    wT, wrest = pl.pallas_call(
        functools.partial(_wprep_kernel, n_transposed=n_t),
        grid=((QKV_WIDTH + REST_WIDTH) // WPREP_COLS,),
        in_specs=[pl.BlockSpec((D_MODEL, WPREP_COLS), lambda c: (0, c))],
        out_specs=[
            pl.BlockSpec((WPREP_COLS, D_MODEL), lambda c: (jnp.minimum(c, n_t - 1), 0)),
            pl.BlockSpec((D_MODEL, WPREP_COLS), lambda c: (0, jnp.maximum(c - n_t, 0))),
        ],
        out_shape=[
            jax.ShapeDtypeStruct((QKV_WIDTH, D_MODEL), bf16),
            jax.ShapeDtypeStruct((D_MODEL, REST_WIDTH), bf16),
        ],
        compiler_params=pltpu.CompilerParams(dimension_semantics=("arbitrary",)),
        name="wprep",
    )(w_in)
    pos_row = positions.reshape(batch, 1, seq)
    pos_tiles = positions.reshape(batch, nk, 1, TK)

    cparams = functools.partial(pltpu.CompilerParams, vmem_limit_bytes=VMEM_LIMIT)

    def rows(width):
        return pl.BlockSpec((TMP, width), lambda b, t: (b * nt + t, 0))

    qT, k, vT, sz, cvs, gates = pl.pallas_call(
        _proj_kernel,
        grid=(batch, nt),
        in_specs=[
            rows(D_MODEL),
            pl.BlockSpec((1, 1, TMP), lambda b, t: (b, 0, t)),
            _resident((ROPE_HALF, 1), lambda b, t: (0, 0)),
            _resident((1, D_MODEL), lambda b, t: (0, 0)),
            _resident((QKV_WIDTH, D_MODEL), lambda b, t: (0, 0)),
            _resident((D_MODEL, REST_WIDTH), lambda b, t: (0, 0)),
            _resident((2, D_MODEL), lambda b, t: (0, 0)),
            _resident((CONV_K, CONV_WIDTH), lambda b, t: (0, 0)),
        ],
        out_specs=[
            pl.BlockSpec((1, QK_WIDTH, TMP), lambda b, t: (b, 0, t)),
            pl.BlockSpec((1, HEADS, TMP, V_DIM), lambda b, t: (b, 0, t, 0)),
            pl.BlockSpec((1, TMP // TK, HEADS * VX_DIM, TK), lambda b, t: (b, t, 0, 0)),
            rows(ATT_WIDTH),
            rows(CONV_WIDTH),
            rows(2 * D_MODEL),
        ],
        out_shape=[
            jax.ShapeDtypeStruct((batch, QK_WIDTH, seq), bf16),
            jax.ShapeDtypeStruct((batch, HEADS, seq, V_DIM), bf16),
            jax.ShapeDtypeStruct((batch, nk, HEADS * VX_DIM, TK), bf16),
            jax.ShapeDtypeStruct((n, ATT_WIDTH), bf16),
            jax.ShapeDtypeStruct((n, CONV_WIDTH), bf16),
            jax.ShapeDtypeStruct((n, 2 * D_MODEL), bf16),
        ],
        scratch_shapes=[pltpu.VMEM((8, CONV_WIDTH), jnp.float32)],
        compiler_params=cparams(dimension_semantics=("arbitrary", "arbitrary")),
        name="proj",
    )(x2, pos_row, freqs, pre_w.reshape(1, D_MODEL), wT, wrest,
      merge_bias, conv_w.reshape(CONV_K, CONV_WIDTH))

    klist, n_unm, n_tot = _key_tile_lists(positions, batch, nq, nk)

    att = pl.pallas_call(
        functools.partial(_attn_kernel, nq=nq, nk=nk, lambda_init=lambda_init),
        grid_spec=pltpu.PrefetchScalarGridSpec(
            num_scalar_prefetch=3,
            grid=(batch, nq),
            in_specs=[
                pl.BlockSpec((1, QK_WIDTH, TQ), lambda b, i, *_: (b, 0, i)),
                pl.BlockSpec((1, HEADS, seq, V_DIM), lambda b, i, *_: (b, 0, 0, 0)),
                pl.BlockSpec((1, nk, HEADS * VX_DIM, TK), lambda b, i, *_: (b, 0, 0, 0)),
                pl.BlockSpec((1, 1, TQ), lambda b, i, *_: (b, 0, i)),
                pl.BlockSpec((1, nk, 1, TK), lambda b, i, *_: (b, 0, 0, 0)),
                _resident((1, HEAD_DIM), lambda b, i, *_: (0, 0)),
                _resident((1, HEAD_DIM), lambda b, i, *_: (0, 0)),
                _resident((1, HEAD_DIM), lambda b, i, *_: (0, 0)),
                _resident((1, HEAD_DIM), lambda b, i, *_: (0, 0)),
                _resident((V_DIM, 1), lambda b, i, *_: (0, 0)),
                pl.BlockSpec((TQ, ATT_WIDTH), lambda b, i, *_: (b * nq + i, 0)),
            ],
            out_specs=pl.BlockSpec((TQ, ATT_WIDTH), lambda b, i, *_: (b * nq + i, 0)),
            scratch_shapes=[
                pltpu.VMEM((HEADS, V_DIM, 2 * TQ), bf16),
                pltpu.VMEM((HEADS, VX_DIM, 2 * TQ), jnp.float32),
                pltpu.VMEM((HEADS, 1, 2 * TQ), jnp.float32),
                pltpu.VMEM((TK, 2 * TQ), jnp.float32),
                pltpu.VMEM((TK, 2 * TQ), jnp.float32),
                pltpu.VMEM((1, 2 * TQ), jnp.float32),
            ],
        ),
        out_shape=jax.ShapeDtypeStruct((n, ATT_WIDTH), bf16),
        compiler_params=cparams(dimension_semantics=("arbitrary", "arbitrary")),
        name="attn",
    )(klist, n_unm, n_tot, qT, k, vT, pos_row, pos_tiles,
      lq1.reshape(1, HEAD_DIM), lk1.reshape(1, HEAD_DIM),
      lq2.reshape(1, HEAD_DIM), lk2.reshape(1, HEAD_DIM), subln_w.reshape(V_DIM, 1), sz)

    out = pl.pallas_call(
        _out_kernel,
        grid=(n // TMO,),
        in_specs=[
            pl.BlockSpec((TMO, D_MODEL), lambda t: (t, 0)),
            pl.BlockSpec((TMO, ATT_WIDTH), lambda t: (t, 0)),
            pl.BlockSpec((TMO, CONV_WIDTH), lambda t: (t, 0)),
            pl.BlockSpec((TMO, 2 * D_MODEL), lambda t: (t, 0)),
            _resident((ATT_WIDTH, D_MODEL), lambda t: (0, 0)),
            _resident((CONV_WIDTH, D_MODEL), lambda t: (0, 0)),
            _resident((D_MODEL, D_MODEL), lambda t: (0, 0)),
            _resident((1, D_MODEL), lambda t: (0, 0)),
        ],
        out_specs=pl.BlockSpec((TMO, D_MODEL), lambda t: (t, 0)),
        out_shape=jax.ShapeDtypeStruct((n, D_MODEL), jnp.float32),
        scratch_shapes=[
            pltpu.VMEM((ATT_WIDTH, D_MODEL), bf16),
            pltpu.VMEM((CONV_WIDTH, D_MODEL), bf16),
            pltpu.VMEM((D_MODEL, D_MODEL), bf16),
        ],
        compiler_params=cparams(dimension_semantics=("arbitrary",)),
        name="outp",
    )(x2, att, cvs, gates, w_att_out, w_conv_out, w_out, post_w.reshape(1, D_MODEL))
    return out


def kernel(x, positions, pre_norm_w, w_in, merge_bias, lambda_q1, lambda_k1, lambda_q2,
           lambda_k2, subln_w, w_att_out, conv_w, w_conv_out, w_out, post_norm_w):
    batch, seq, _ = x.shape
    depth = w_in.shape[0]
    x2 = x.reshape(batch * seq, D_MODEL)
    for layer in range(depth):
        lambda_init = 0.8 - 0.6 * math.exp(-0.3 * layer)
        x2 = _layer(x2, positions, pre_norm_w[layer], w_in[layer], merge_bias[layer],
                    lambda_q1[layer], lambda_k1[layer], lambda_q2[layer], lambda_k2[layer],
                    subln_w[layer], w_att_out[layer], conv_w[layer], w_conv_out[layer],
                    w_out[layer], post_norm_w[layer], lambda_init, batch, seq)
    return x2.reshape(batch, seq, D_MODEL)
```

```python
import functools
import math

import jax
import jax.numpy as jnp
from jax import lax
from jax.experimental import pallas as pl
from jax.experimental.pallas import tpu as pltpu

D_MODEL = 1024
CHUNK_SHIFT = 6
HEADS = 4
HEAD_DIM = 64
V_DIM = 2 * HEAD_DIM
LANES = 128
BF16_ROWS = 16
VX_DIM = V_DIM + BF16_ROWS
QK_WIDTH = 2 * HEADS * HEAD_DIM
ATT_WIDTH = HEADS * V_DIM
CONV_WIDTH = D_MODEL // 2
CONV_K = 3
ROPE_THETA = 500000.0
ROPE_DIM = HEAD_DIM // 4
ROPE_HALF = ROPE_DIM // 2
NORM_EPS = 1e-6
SUBLN_EPS = 1e-5
REST_WIDTH = ATT_WIDTH + 4 * CONV_WIDTH + 2 * D_MODEL
QKV_WIDTH = 2 * QK_WIDTH + ATT_WIDTH
WPREP_COLS = 512

TM = 256
TMP = 1024
TQ = 512
TK = 512
FREE_TILE_GROUPS = (2, 1)
TMO = 1024
OUT_ROWS = 256

VMEM_LIMIT = 56 * 1024 * 1024
NEG = float(jnp.finfo(jnp.float32).min) / 2

_NT = (((1,), (1,)), ((), ()))


def _resident(block_shape, index_map):
    return pl.BlockSpec(block_shape, index_map, pipeline_mode=pl.Buffered(1))


def _wprep_kernel(w_ref, wT_ref, wrest_ref, *, n_transposed):
    c = pl.program_id(0)

    @pl.when(c < n_transposed)
    def _():
        wT_ref[...] = w_ref[...].T.astype(jnp.bfloat16)

    @pl.when(c >= n_transposed)
    def _():
        wrest_ref[...] = w_ref[...].astype(jnp.bfloat16)


def _proj_kernel(x_ref, pos_ref, freq_ref, prew_ref, wT_ref, wrest_ref, mb_ref, cw_ref,
                 qT_ref, k_ref, vT_ref, sz_ref, cvs_ref, gate_ref, cu_prev_ref):
    f32 = jnp.float32
    cw = CONV_WIDTH

    @pl.when(pl.program_id(1) == 0)
    def _():
        cu_prev_ref[...] = jnp.zeros(cu_prev_ref.shape, f32)

    def sigmoid(v):
        return 0.5 * jnp.tanh(0.5 * v) + 0.5

    def chain(ci, prev):
        rows = slice(ci * TM, (ci + 1) * TM)
        x = x_ref[rows, :]
        ms = jnp.mean(x * x, axis=-1, keepdims=True)
        h = (x * lax.rsqrt(ms + NORM_EPS)) * prew_ref[...]
        hb = h.astype(jnp.bfloat16)

        def cols(c0, width):
            return jnp.dot(hb, wrest_ref[:, c0:c0 + width], preferred_element_type=f32)

        c0 = ATT_WIDTH + 4 * cw
        for c in range(0, 2 * D_MODEL, 512):
            bias = mb_ref[c // D_MODEL:c // D_MODEL + 1, c % D_MODEL:c % D_MODEL + 512]
            gate_ref[rows, c:c + 512] = sigmoid(cols(c0 + c, 512) + bias).astype(jnp.bfloat16)

        z_att = cols(0, ATT_WIDTH)
        sz_ref[rows, :] = (z_att * sigmoid(z_att)).astype(jnp.bfloat16)

        c0 = ATT_WIDTH
        gb = cols(c0, cw)
        cu = cols(c0 + cw, cw) * cols(c0 + 2 * cw, cw)
        z_conv = cols(c0 + 3 * cw, cw)

        h1 = prev[7:8]
        h2 = prev[6:7]
        row = lax.broadcasted_iota(jnp.int32, (8, cw), 0)

        def shifted(by, fix_top):
            r = pltpu.roll(cu, by, 0)
            return jnp.concatenate([fix_top(r[0:8]), r[8:]], axis=0)

        r1 = shifted(1, lambda top: jnp.where(row == 0, h1, top))
        r2 = shifted(2, lambda top: jnp.where(row == 0, h2, jnp.where(row == 1, h1, top)))
        conv = cw_ref[0:1, :] * r2 + cw_ref[1:2, :] * r1 + cw_ref[2:3, :] * cu
        cvs_ref[rows, :] = ((gb * conv) * (z_conv * sigmoid(z_conv))).astype(jnp.bfloat16)

        qkT = lax.dot_general(wT_ref[0:2 * QK_WIDTH, :], hb, _NT, preferred_element_type=f32)

        pos = pos_ref[0, :, rows].astype(jnp.float32)
        ang = freq_ref[...] * pos
        cos = jnp.cos(ang)
        sin = jnp.sin(ang)

        def rot(blk):
            t1 = blk[0:ROPE_HALF]
            t2 = blk[ROPE_HALF:ROPE_DIM]
            return jnp.concatenate(
                [t1 * cos - t2 * sin, t2 * cos + t1 * sin, blk[ROPE_DIM:]], axis=0)

        scale = math.log2(math.e) / math.sqrt(HEAD_DIM)
        for hd in range(2 * HEADS):
            r0 = hd * HEAD_DIM
            qT_ref[0, r0:r0 + HEAD_DIM, rows] = (
                rot(qkT[r0:r0 + HEAD_DIM]) * scale).astype(jnp.bfloat16)
        kT = jnp.concatenate(
            [rot(qkT[QK_WIDTH + hd * HEAD_DIM:QK_WIDTH + (hd + 1) * HEAD_DIM])
             for hd in range(2 * HEADS)], axis=0)
        k = kT.T.astype(jnp.bfloat16)
        for p in range(HEADS):
            k_ref[0, p, rows, :] = k[:, p * V_DIM:(p + 1) * V_DIM]

        vT = lax.dot_general(wT_ref[2 * QK_WIDTH:QKV_WIDTH, :], hb, _NT,
                             preferred_element_type=f32)
        one_row = lax.broadcasted_iota(jnp.int32, (BF16_ROWS, TM), 0) == 0
        ones_rows = jnp.where(one_row, 1.0, 0.0).astype(jnp.bfloat16)
        for p in range(HEADS):
            tile, lanes = divmod(ci * TM, TK)
            vT_ref[0, tile, p * VX_DIM:p * VX_DIM + V_DIM, lanes:lanes + TM] = (
                vT[p * V_DIM:(p + 1) * V_DIM].astype(jnp.bfloat16))
            vT_ref[0, tile, p * VX_DIM + V_DIM:(p + 1) * VX_DIM, lanes:lanes + TM] = ones_rows
        return cu[TM - 8:TM]

    prev = cu_prev_ref[...]
    for ci in range(TMP // TM):
        prev = chain(ci, prev)
    cu_prev_ref[...] = prev


def _attn_kernel(klist_ref, nunm_ref, ntot_ref,
                 qT_ref, k_ref, vT_ref, posr_ref, posk_ref,
                 lq1_ref, lk1_ref, lq2_ref, lk2_ref, sw_ref, sz_ref,
                 o_ref, q2_ref, acc_ref, m_ref, sa_ref, sb_ref, mt0_ref,
                 *, nq, nk, lambda_init):
    qt = pl.program_id(0) * nq + pl.program_id(1)
    n = ntot_ref[qt]
    n_unm = nunm_ref[qt]

    qc = lax.shift_right_arithmetic(posr_ref[0], CHUNK_SHIFT)
    qc2 = jnp.concatenate([qc, qc], axis=1)

    def init_state():
        for p in range(HEADS):
            q2_ref[p] = jnp.zeros(q2_ref.shape[1:], q2_ref.dtype)
            q2_ref[p, 0:HEAD_DIM, 0:TQ] = qT_ref[0, p * V_DIM:p * V_DIM + HEAD_DIM, :]
            q2_ref[p, HEAD_DIM:V_DIM, TQ:2 * TQ] = (
                qT_ref[0, p * V_DIM + HEAD_DIM:(p + 1) * V_DIM, :])
            m_ref[p] = jnp.full(m_ref.shape[1:], NEG, jnp.float32)
            acc_ref[p] = jnp.zeros(acc_ref.shape[1:], jnp.float32)

    def key_tile(r):
        return klist_ref[qt * nk + r]

    def score(p, j, s_ref, masked):
        k0 = pl.multiple_of(j * TK, TK)
        s = jnp.dot(k_ref[0, p, pl.ds(k0, TK), :], q2_ref[p],
                    preferred_element_type=jnp.float32)
        if masked:
            kc = lax.shift_right_arithmetic(posk_ref[0, j], CHUNK_SHIFT)
            kc = jnp.broadcast_to(kc, (LANES, TK)).T
            kc = jnp.concatenate([kc] * (2 * TQ // LANES), axis=1)
            s = jnp.where(kc <= qc2, s, NEG)
        s_ref[...] = s
        return jnp.max(s, axis=0, keepdims=True)

    def update(p, j, s_ref, mt):
        m_old = m_ref[p]
        m_new = jnp.maximum(m_old, mt)
        alpha = jnp.exp2(m_old - m_new)
        pr = jnp.exp2(s_ref[...] - m_new).astype(jnp.bfloat16)
        vt = vT_ref[0, j, p * VX_DIM:(p + 1) * VX_DIM, :]
        acc_ref[p] = alpha * acc_ref[p] + jnp.dot(vt, pr, preferred_element_type=jnp.float32)
        m_ref[p] = m_new

    def block(j, j_next, masked, next_masked):
        mt1 = score(1, j, sb_ref, masked)
        update(0, j, sa_ref, mt0_ref[...])
        mt2 = score(2, j, sa_ref, masked)
        update(1, j, sb_ref, mt1)
        mt3 = score(3, j, sb_ref, masked)
        update(2, j, sa_ref, mt2)
        if next_masked is not None:
            mt0_ref[...] = score(0, j_next, sa_ref, next_masked)
        update(3, j, sb_ref, mt3)

    @pl.when(n_unm > 0)
    def _():
        init_state()
        mt0_ref[...] = score(0, key_tile(0), sa_ref, False)

    @pl.when(n_unm == 0)
    def _():
        init_state()
        mt0_ref[...] = score(0, key_tile(0), sa_ref, True)

    done = 0
    for group in FREE_TILE_GROUPS:
        n_groups = jnp.maximum(n_unm - 1 - done, 0) // group

        def free_blocks(g, carry, group=group, done=done):
            r0 = done + g * group
            for r in range(group):
                block(key_tile(r0 + r), key_tile(r0 + r + 1), False, False)
            return carry

        lax.fori_loop(0, n_groups, free_blocks, 0)
        done = done + n_groups * group

    @pl.when((n_unm > 0) & (n > n_unm))
    def _():
        block(key_tile(n_unm - 1), key_tile(n_unm), False, True)

    @pl.when((n_unm > 0) & (n == n_unm))
    def _():
        block(key_tile(n_unm - 1), None, False, None)

    def masked_block(r, carry):
        block(key_tile(r), key_tile(r + 1), True, True)
        return carry

    lax.fori_loop(n_unm, n - 1, masked_block, 0)

    @pl.when(n > n_unm)
    def _():
        block(key_tile(n - 1), None, True, None)

    lam = (jnp.exp(jnp.sum(lq1_ref[...] * lk1_ref[...], axis=-1, keepdims=True))
           - jnp.exp(jnp.sum(lq2_ref[...] * lk2_ref[...], axis=-1, keepdims=True))
           + lambda_init)
    for p in range(HEADS):
        inv = 1.0 / acc_ref[p, V_DIM:V_DIM + 1, :]
        o = acc_ref[p, 0:V_DIM, :] * inv
        d = o[:, :TQ] - lam * o[:, TQ:]
        ms = jnp.mean(d * d, axis=0, keepdims=True)
        dn = ((d * lax.rsqrt(ms + SUBLN_EPS)) * sw_ref[...]) * (1.0 - lambda_init)
        lanes = slice(p * V_DIM, (p + 1) * V_DIM)
        o_ref[:, lanes] = (dn.T * sz_ref[:, lanes].astype(jnp.float32)).astype(o_ref.dtype)


def _out_kernel(x_ref, att_ref, cvs_ref, gate_ref, wao_ref, wco_ref, wo_ref, postw_ref, o_ref):
    f32 = jnp.float32
    for r0 in range(0, TMO, OUT_ROWS):
        rows = slice(r0, r0 + OUT_ROWS)
        y_att = jnp.dot(att_ref[rows, :], wao_ref[...], preferred_element_type=f32)
        y_conv = jnp.dot(cvs_ref[rows, :], wco_ref[...], preferred_element_type=f32)
        m = (gate_ref[rows, 0:D_MODEL].astype(f32) * y_att
             + gate_ref[rows, D_MODEL:2 * D_MODEL].astype(f32) * y_conv)
        o = jnp.dot(m.astype(jnp.bfloat16), wo_ref[...], preferred_element_type=f32)
        ms = jnp.mean(o * o, axis=-1, keepdims=True)
        o_ref[rows, :] = x_ref[rows, :] + (o * lax.rsqrt(ms + NORM_EPS)) * postw_ref[...]


def _key_tile_lists(positions, batch, nq, nk):
    chunk = lax.shift_right_arithmetic(positions, CHUNK_SHIFT)
    kch = chunk.reshape(batch, 1, nk, TK)
    qch = chunk.reshape(batch, nq, 1, TQ)
    needed = kch.min(-1) <= qch.max(-1)
    nomask = needed & (kch.max(-1) <= qch.min(-1))
    rank = jnp.where(nomask, 0, jnp.where(needed, 1, 2))
    order = jnp.argsort(rank, axis=-1, stable=True).astype(jnp.int32)

    def count(flag):
        return flag.sum(-1).astype(jnp.int32).reshape(-1)

    return order.reshape(-1), count(nomask), count(needed)


def _layer(x2, positions, pre_w, w_in, merge_bias, lq1, lk1, lq2, lk2, subln_w,
           w_att_out, conv_w, w_conv_out, w_out, post_w, lambda_init, batch, seq):
    bf16 = jnp.bfloat16
    n = batch * seq
    nt = seq // TMP
    nq = seq // TQ
    nk = seq // TK
    assert seq % TMP == 0 and seq % TQ == 0 and TK % TM == 0 and n % TMO == 0

    freqs = (ROPE_THETA ** (-jnp.arange(0, ROPE_DIM, 2, dtype=jnp.float32) / ROPE_DIM)
             ).reshape(ROPE_HALF, 1)
    n_t = QKV_WIDTH // WPREP_COLS
    wT, wrest = pl.pallas_call(
        functools.partial(_wprep_kernel, n_transposed=n_t),
        grid=((QKV_WIDTH + REST_WIDTH) // WPREP_COLS,),
        in_specs=[pl.BlockSpec((D_MODEL, WPREP_COLS), lambda c: (0, c))],
        out_specs=[
            pl.BlockSpec((WPREP_COLS, D_MODEL), lambda c: (jnp.minimum(c, n_t - 1), 0)),
            pl.BlockSpec((D_MODEL, WPREP_COLS), lambda c: (0, jnp.maximum(c - n_t, 0))),
        ],
        out_shape=[
            jax.ShapeDtypeStruct((QKV_WIDTH, D_MODEL), bf16),
            jax.ShapeDtypeStruct((D_MODEL, REST_WIDTH), bf16),
        ],
        compiler_params=pltpu.CompilerParams(dimension_semantics=("arbitrary",)),
        name="wprep",
    )(w_in)
    pos_row = positions.reshape(batch, 1, seq)
    pos_tiles = positions.reshape(batch, nk, 1, TK)

    cparams = functools.partial(pltpu.CompilerParams, vmem_limit_bytes=VMEM_LIMIT)

    def rows(width):
        return pl.BlockSpec((TMP, width), lambda b, t: (b * nt + t, 0))

    qT, k, vT, sz, cvs, gates = pl.pallas_call(
        _proj_kernel,
        grid=(batch, nt),
        in_specs=[
            rows(D_MODEL),
            pl.BlockSpec((1, 1, TMP), lambda b, t: (b, 0, t)),
            _resident((ROPE_HALF, 1), lambda b, t: (0, 0)),
            _resident((1, D_MODEL), lambda b, t: (0, 0)),
            _resident((QKV_WIDTH, D_MODEL), lambda b, t: (0, 0)),
            _resident((D_MODEL, REST_WIDTH), lambda b, t: (0, 0)),
            _resident((2, D_MODEL), lambda b, t: (0, 0)),
            _resident((CONV_K, CONV_WIDTH), lambda b, t: (0, 0)),
        ],
        out_specs=[
            pl.BlockSpec((1, QK_WIDTH, TMP), lambda b, t: (b, 0, t)),
            pl.BlockSpec((1, HEADS, TMP, V_DIM), lambda b, t: (b, 0, t, 0)),
            pl.BlockSpec((1, TMP // TK, HEADS * VX_DIM, TK), lambda b, t: (b, t, 0, 0)),
            rows(ATT_WIDTH),
            rows(CONV_WIDTH),
            rows(2 * D_MODEL),
        ],
        out_shape=[
            jax.ShapeDtypeStruct((batch, QK_WIDTH, seq), bf16),
            jax.ShapeDtypeStruct((batch, HEADS, seq, V_DIM), bf16),
            jax.ShapeDtypeStruct((batch, nk, HEADS * VX_DIM, TK), bf16),
            jax.ShapeDtypeStruct((n, ATT_WIDTH), bf16),
            jax.ShapeDtypeStruct((n, CONV_WIDTH), bf16),
            jax.ShapeDtypeStruct((n, 2 * D_MODEL), bf16),
        ],
        scratch_shapes=[pltpu.VMEM((8, CONV_WIDTH), jnp.float32)],
        compiler_params=cparams(dimension_semantics=("arbitrary", "arbitrary")),
        name="proj",
    )(x2, pos_row, freqs, pre_w.reshape(1, D_MODEL), wT, wrest,
      merge_bias, conv_w.reshape(CONV_K, CONV_WIDTH))

    klist, n_unm, n_tot = _key_tile_lists(positions, batch, nq, nk)

    att = pl.pallas_call(
        functools.partial(_attn_kernel, nq=nq, nk=nk, lambda_init=lambda_init),
        grid_spec=pltpu.PrefetchScalarGridSpec(
            num_scalar_prefetch=3,
            grid=(batch, nq),
            in_specs=[
                pl.BlockSpec((1, QK_WIDTH, TQ), lambda b, i, *_: (b, 0, i)),
                pl.BlockSpec((1, HEADS, seq, V_DIM), lambda b, i, *_: (b, 0, 0, 0)),
                pl.BlockSpec((1, nk, HEADS * VX_DIM, TK), lambda b, i, *_: (b, 0, 0, 0)),
                pl.BlockSpec((1, 1, TQ), lambda b, i, *_: (b, 0, i)),
                pl.BlockSpec((1, nk, 1, TK), lambda b, i, *_: (b, 0, 0, 0)),
                _resident((1, HEAD_DIM), lambda b, i, *_: (0, 0)),
                _resident((1, HEAD_DIM), lambda b, i, *_: (0, 0)),
                _resident((1, HEAD_DIM), lambda b, i, *_: (0, 0)),
                _resident((1, HEAD_DIM), lambda b, i, *_: (0, 0)),
                _resident((V_DIM, 1), lambda b, i, *_: (0, 0)),
                pl.BlockSpec((TQ, ATT_WIDTH), lambda b, i, *_: (b * nq + i, 0)),
            ],
            out_specs=pl.BlockSpec((TQ, ATT_WIDTH), lambda b, i, *_: (b * nq + i, 0)),
            scratch_shapes=[
                pltpu.VMEM((HEADS, V_DIM, 2 * TQ), bf16),
                pltpu.VMEM((HEADS, VX_DIM, 2 * TQ), jnp.float32),
                pltpu.VMEM((HEADS, 1, 2 * TQ), jnp.float32),
                pltpu.VMEM((TK, 2 * TQ), jnp.float32),
                pltpu.VMEM((TK, 2 * TQ), jnp.float32),
                pltpu.VMEM((1, 2 * TQ), jnp.float32),
            ],
        ),
        out_shape=jax.ShapeDtypeStruct((n, ATT_WIDTH), bf16),
        compiler_params=cparams(dimension_semantics=("arbitrary", "arbitrary")),
        name="attn",
    )(klist, n_unm, n_tot, qT, k, vT, pos_row, pos_tiles,
      lq1.reshape(1, HEAD_DIM), lk1.reshape(1, HEAD_DIM),
      lq2.reshape(1, HEAD_DIM), lk2.reshape(1, HEAD_DIM), subln_w.reshape(V_DIM, 1), sz)

    out = pl.pallas_call(
        _out_kernel,
        grid=(n // TMO,),
        in_specs=[
            pl.BlockSpec((TMO, D_MODEL), lambda t: (t, 0)),
            pl.BlockSpec((TMO, ATT_WIDTH), lambda t: (t, 0)),
            pl.BlockSpec((TMO, CONV_WIDTH), lambda t: (t, 0)),
            pl.BlockSpec((TMO, 2 * D_MODEL), lambda t: (t, 0)),
            _resident((ATT_WIDTH, D_MODEL), lambda t: (0, 0)),
            _resident((CONV_WIDTH, D_MODEL), lambda t: (0, 0)),
            _resident((D_MODEL, D_MODEL), lambda t: (0, 0)),
            _resident((1, D_MODEL), lambda t: (0, 0)),
        ],
        out_specs=pl.BlockSpec((TMO, D_MODEL), lambda t: (t, 0)),
        out_shape=jax.ShapeDtypeStruct((n, D_MODEL), jnp.float32),
        compiler_params=cparams(dimension_semantics=("arbitrary",)),
        name="outp",
    )(x2, att, cvs, gates, w_att_out.astype(bf16), w_conv_out.astype(bf16), w_out.astype(bf16),
      post_w.reshape(1, D_MODEL))
    return out


def kernel(x, positions, pre_norm_w, w_in, merge_bias, lambda_q1, lambda_k1, lambda_q2,
           lambda_k2, subln_w, w_att_out, conv_w, w_conv_out, w_out, post_norm_w):
    batch, seq, _ = x.shape
    depth = w_in.shape[0]
    x2 = x.reshape(batch * seq, D_MODEL)
    for layer in range(depth):
        lambda_init = 0.8 - 0.6 * math.exp(-0.3 * layer)
        x2 = _layer(x2, positions, pre_norm_w[layer], w_in[layer], merge_bias[layer],
                    lambda_q1[layer], lambda_k1[layer], lambda_q2[layer], lambda_k2[layer],
                    subln_w[layer], w_att_out[layer], conv_w[layer], w_conv_out[layer],
                    w_out[layer], post_norm_w[layer], lambda_init, batch, seq)
    return x2.reshape(batch, seq, D_MODEL)
```
